```python
import jax, jax.numpy as jnp
from jax import lax
import numpy as np

D_MODEL = 1024
BATCH = 2
SEQ = 16384
DEPTH = 1
DEC_BATCH = 32
DEC_SEQ = 2048
PAST_LEN = 128

N_HEADS = 8
N_KV_HEADS = 2
HEAD_DIM = 128
GROUP = N_HEADS // N_KV_HEADS
WINDOW = 128
BLOCK = 128
SPAN = BLOCK + 2 * WINDOW
Q_W = N_HEADS * HEAD_DIM
KV_W = N_KV_HEADS * HEAD_DIM
NEG_INF = -1e30
D_RNN = 1024
N_RNN_BLOCKS = 8
RNN_BLOCK = D_RNN // N_RNN_BLOCKS
CONV_WIDTH = 4
CONV_LEFT = 2
LRU_C = 8.0
D_FF = 2816
EPS = 1e-6
D_IN = Q_W + 2 * KV_W + 2 * D_RNN + 2 * D_MODEL
SPLIT_POINTS = (Q_W, Q_W + KV_W, Q_W + 2 * KV_W, Q_W + 2 * KV_W + D_RNN,
                Q_W + 2 * KV_W + 2 * D_RNN, Q_W + 2 * KV_W + 2 * D_RNN + D_MODEL)

kernel_name = "hybrid_bidir_local_gqa_rglru_macaron"


def rmsnorm(x, g):
    xf = x.astype(jnp.float32)
    y = xf * lax.rsqrt(jnp.mean(xf * xf, axis=-1, keepdims=True) + EPS)
    return (y * g.astype(jnp.float32)).astype(x.dtype)


def swiglu(x, w_gate, w_up, w_down):
    return (jax.nn.silu(x @ w_gate) * (x @ w_up)) @ w_down


def alibi_slopes():
    return 2.0 ** (-8.0 * jnp.arange(1, N_HEADS + 1, dtype=jnp.float32) / N_HEADS)


def local_attention(q, k, v, sink):
    B, S, _ = q.shape
    nblk = S // BLOCK
    qb = jnp.moveaxis(q.reshape(B, nblk, BLOCK, N_KV_HEADS, GROUP, HEAD_DIM), 1, 0)
    pad = ((0, 0), (WINDOW, WINDOW), (0, 0), (0, 0))
    kp = jnp.pad(k.reshape(B, S, N_KV_HEADS, HEAD_DIM), pad)
    vp = jnp.pad(v.reshape(B, S, N_KV_HEADS, HEAD_DIM), pad)
    qi = jnp.arange(BLOCK)[:, None]
    kc = jnp.arange(SPAN)[None, :]
    dist = jnp.abs(qi + WINDOW - kc)
    band = dist <= WINDOW
    slopes = alibi_slopes().reshape(N_KV_HEADS, GROUP)
    bias = -slopes[:, :, None, None] * dist.astype(jnp.float32)[None, None]
    sink_f = sink.astype(jnp.float32).reshape(N_KV_HEADS, GROUP)[:, :, None, None]
    scale = HEAD_DIM ** -0.5

    def one_block(args):
        q_blk, j = args
        k_blk = lax.dynamic_slice_in_dim(kp, j * BLOCK, SPAN, axis=1)
        v_blk = lax.dynamic_slice_in_dim(vp, j * BLOCK, SPAN, axis=1)
        s = jnp.einsum("bqkgd,bskd->bkgqs", q_blk, k_blk,
                       preferred_element_type=jnp.float32) * scale + bias
        key_pos = j * BLOCK - WINDOW + jnp.arange(SPAN)
        valid = band & ((key_pos >= 0) & (key_pos < S))[None, :]
        s = jnp.where(valid, s, NEG_INF)
        m = jnp.maximum(jnp.max(s, axis=-1, keepdims=True), sink_f)
        p = jnp.exp(s - m)
        denom = jnp.sum(p, axis=-1, keepdims=True) + jnp.exp(sink_f - m)
        return jnp.einsum("bkgqs,bskd->bqkgd", (p / denom).astype(v_blk.dtype), v_blk)

    out = lax.map(one_block, (qb, jnp.arange(nblk)))
    return jnp.moveaxis(out, 0, 1).reshape(B, S, Q_W)


def centred_conv(x, w, b):
    S = x.shape[1]
    xp = jnp.pad(x, ((0, 0), (CONV_LEFT, CONV_WIDTH - 1 - CONV_LEFT), (0, 0)))
    out = b + xp[:, 0:S] * w[0]
    for tap in range(1, CONV_WIDTH):
        out = out + xp[:, tap:tap + S] * w[tap]
    return out


def _lin_combine(e1, e2):
    a1, b1 = e1
    a2, b2 = e2
    return a1 * a2, a2 * b1 + b2


def rglru(x, w_a, b_a, w_i, b_i, lam):
    B, S, _ = x.shape
    xb = x.reshape(B, S, N_RNN_BLOCKS, RNN_BLOCK)
    r = jax.nn.sigmoid(jnp.einsum("bsnc,ncd->bsnd", xb, w_a).reshape(B, S, D_RNN) + b_a)
    ig = jax.nn.sigmoid(jnp.einsum("bsnc,ncd->bsnd", xb, w_i).reshape(B, S, D_RNN) + b_i)
    log_a = -LRU_C * r.astype(jnp.float32) * jax.nn.softplus(-lam.astype(jnp.float32))
    a = jnp.exp(log_a)
    u = jnp.sqrt(-jnp.expm1(2.0 * log_a)) * (ig * x).astype(jnp.float32)
    _, h = lax.associative_scan(_lin_combine, (a, u), axis=1)
    return h


def token_mixer(h, w_in, attn_sink, conv_w, conv_b,
                wa_f, ba_f, wi_f, bi_f, lam_f, wa_b, ba_b, wi_b, bi_b, lam_b,
                w_attn_o, w_rnn_o, w_out):
    z = h @ w_in
    q, k, v, xr, yr, g_attn, g_rnn = jnp.split(z, SPLIT_POINTS, axis=-1)
    attn = local_attention(q, k, v, attn_sink) @ w_attn_o
    xc = centred_conv(xr, conv_w, conv_b)
    h_fwd = rglru(xc, wa_f, ba_f, wi_f, bi_f, lam_f)
    h_bwd = jnp.flip(rglru(jnp.flip(xc, axis=1), wa_b, ba_b, wi_b, bi_b, lam_b), axis=1)
    hr = (h_fwd + h_bwd).astype(h.dtype)
    rnn = (hr * jax.nn.gelu(yr)) @ w_rnn_o
    merged = jax.nn.sigmoid(g_attn) * attn + jax.nn.sigmoid(g_rnn) * rnn
    return merged @ w_out


def trunk(x, lw, norm_final):
    (norm_ffn1, ffn1_gate, ffn1_up, ffn1_down, norm_mix, w_in, attn_sink, conv_w, conv_b,
     wa_f, ba_f, wi_f, bi_f, lam_f, wa_b, ba_b, wi_b, bi_b, lam_b,
     w_attn_o, w_rnn_o, w_out, norm_ffn2, ffn2_gate, ffn2_up, ffn2_down) = lw
    for l in range(DEPTH):
        x = x + 0.5 * swiglu(rmsnorm(x, norm_ffn1[l]), ffn1_gate[l], ffn1_up[l], ffn1_down[l])
        x = x + token_mixer(rmsnorm(x, norm_mix[l]), w_in[l], attn_sink[l], conv_w[l], conv_b[l],
                            wa_f[l], ba_f[l], wi_f[l], bi_f[l], lam_f[l],
                            wa_b[l], ba_b[l], wi_b[l], bi_b[l], lam_b[l],
                            w_attn_o[l], w_rnn_o[l], w_out[l])
        x = x + 0.5 * swiglu(rmsnorm(x, norm_ffn2[l]), ffn2_gate[l], ffn2_up[l], ffn2_down[l])
    return rmsnorm(x, norm_final)


def setup_inputs(seed: int = 0) -> dict:
    key = jax.random.key(seed)
    ks = iter(jax.random.split(key, 40))
    f32 = jnp.float32

    def nrm(shape, scale):
        return jax.random.normal(next(ks), shape, f32) * scale

    def gain(shape):
        return 1.0 + 0.05 * jax.random.normal(next(ks), shape, f32)

    def lam_init():
        a0 = jax.random.uniform(next(ks), (DEPTH, D_RNN), f32, 0.9, 0.999)
        p = a0 ** (1.0 / LRU_C)
        return jnp.log(p) - jnp.log1p(-p)

    L = DEPTH
    return {
        "x_prompt": jax.random.normal(next(ks), (BATCH, SEQ, D_MODEL), f32),
        "x_sample": jax.random.normal(next(ks), (DEC_BATCH, DEC_SEQ, D_MODEL), f32),
        "norm_ffn1": gain((L, D_MODEL)),
        "ffn1_gate": nrm((L, D_MODEL, D_FF), D_MODEL ** -0.5),
        "ffn1_up": nrm((L, D_MODEL, D_FF), D_MODEL ** -0.5),
        "ffn1_down": nrm((L, D_FF, D_MODEL), D_FF ** -0.5),
        "norm_mix": gain((L, D_MODEL)),
        "w_in": nrm((L, D_MODEL, D_IN), D_MODEL ** -0.5),
        "attn_sink": nrm((L, N_HEADS), 0.5),
        "conv_w": nrm((L, CONV_WIDTH, D_RNN), CONV_WIDTH ** -0.5),
        "conv_b": nrm((L, D_RNN), 0.02),
        "lru_wa_f": nrm((L, N_RNN_BLOCKS, RNN_BLOCK, RNN_BLOCK), RNN_BLOCK ** -0.5),
        "lru_ba_f": nrm((L, D_RNN), 0.1),
        "lru_wi_f": nrm((L, N_RNN_BLOCKS, RNN_BLOCK, RNN_BLOCK), RNN_BLOCK ** -0.5),
        "lru_bi_f": nrm((L, D_RNN), 0.1),
        "lru_lam_f": lam_init(),
        "lru_wa_b": nrm((L, N_RNN_BLOCKS, RNN_BLOCK, RNN_BLOCK), RNN_BLOCK ** -0.5),
        "lru_ba_b": nrm((L, D_RNN), 0.1),
        "lru_wi_b": nrm((L, N_RNN_BLOCKS, RNN_BLOCK, RNN_BLOCK), RNN_BLOCK ** -0.5),
        "lru_bi_b": nrm((L, D_RNN), 0.1),
        "lru_lam_b": lam_init(),
        "w_attn_o": nrm((L, Q_W, D_MODEL), Q_W ** -0.5),
        "w_rnn_o": nrm((L, D_RNN, D_MODEL), D_RNN ** -0.5),
        "w_out": nrm((L, D_MODEL, D_MODEL), D_MODEL ** -0.5),
        "norm_ffn2": gain((L, D_MODEL)),
        "ffn2_gate": nrm((L, D_MODEL, D_FF), D_MODEL ** -0.5),
        "ffn2_up": nrm((L, D_MODEL, D_FF), D_MODEL ** -0.5),
        "ffn2_down": nrm((L, D_FF, D_MODEL), D_FF ** -0.5),
        "norm_final": gain((D_MODEL,)),
    }


def reference(x_prompt, x_sample, norm_ffn1, ffn1_gate, ffn1_up, ffn1_down, norm_mix, w_in,
              attn_sink, conv_w, conv_b, lru_wa_f, lru_ba_f, lru_wi_f, lru_bi_f, lru_lam_f,
              lru_wa_b, lru_ba_b, lru_wi_b, lru_bi_b, lru_lam_b, w_attn_o, w_rnn_o, w_out,
              norm_ffn2, ffn2_gate, ffn2_up, ffn2_down, norm_final):
    lw = (norm_ffn1, ffn1_gate, ffn1_up, ffn1_down, norm_mix, w_in, attn_sink, conv_w, conv_b,
          lru_wa_f, lru_ba_f, lru_wi_f, lru_bi_f, lru_lam_f,
          lru_wa_b, lru_ba_b, lru_wi_b, lru_bi_b, lru_lam_b,
          w_attn_o, w_rnn_o, w_out, norm_ffn2, ffn2_gate, ffn2_up, ffn2_down)
    y_prompt = trunk(x_prompt, lw, norm_final)
    y_sample = trunk(x_sample, lw, norm_final)
    return (y_prompt, y_sample)
```

```python
import functools

import jax
import jax.numpy as jnp
from jax import lax
from jax.experimental import pallas as pl
from jax.experimental.pallas import tpu as pltpu

D_MODEL = 1024
N_HEADS = 8
N_KV_HEADS = 2
HEAD_DIM = 128
GROUP = N_HEADS // N_KV_HEADS
WINDOW = 128
Q_W = N_HEADS * HEAD_DIM
KV_W = N_KV_HEADS * HEAD_DIM
NEG_INF = -1e30
D_RNN = 1024
N_RNN_BLOCKS = 8
RNN_BLOCK = D_RNN // N_RNN_BLOCKS
CONV_WIDTH = 4
CONV_LEFT = 2
LRU_C = 8.0
D_FF = 2816
EPS = 1e-6

V7X_LANES = 128
V7X_SUBLANES = 8
V7X_MXU_DIM = 256
V7X_VMEM_LIMIT_BYTES = 56 * 1024 * 1024

TOKEN_TILE = 512
FF_CHUNK = V7X_MXU_DIM
Q_TILE = 256
SPAN = Q_TILE + 2 * WINDOW
RNN_TILE = 512
HALO = V7X_SUBLANES

BF16 = jnp.bfloat16
F32 = jnp.float32


def _params(*semantics):
    return pltpu.CompilerParams(dimension_semantics=semantics, vmem_limit_bytes=V7X_VMEM_LIMIT_BYTES)


def _resident(shape):
    return pl.BlockSpec(shape, lambda *_: (0,) * len(shape), pipeline_mode=pl.Buffered(1))


def _rmsnorm(x, g):
    y = x * lax.rsqrt(jnp.mean(x * x, axis=-1, keepdims=True) + EPS)
    return y * g


def _dot(a, b):
    return jnp.dot(a, b, preferred_element_type=F32)


def _ffn_kernel(x_ref, g_ref, wg_ref, wu_ref, wd_ref, *rest, final_norm):
    if final_norm:
        gf_ref, o_ref = rest
    else:
        (o_ref,) = rest
    x = x_ref[...]
    h = _rmsnorm(x, g_ref[...]).astype(BF16)
    acc = jnp.zeros(x.shape, F32)
    for c in range(D_FF // FF_CHUNK):
        cols = slice(c * FF_CHUNK, (c + 1) * FF_CHUNK)
        gate = _dot(h, wg_ref[:, cols])
        up = _dot(h, wu_ref[:, cols])
        act = (jax.nn.silu(gate) * up).astype(BF16)
        acc = acc + _dot(act, wd_ref[cols, :])
    y = x + 0.5 * acc
    if final_norm:
        y = _rmsnorm(y, gf_ref[...])
    o_ref[...] = y


def _ffn(x, g, wg, wu, wd, g_final=None):
    n, d = x.shape
    final_norm = g_final is not None
    tile = pl.BlockSpec((TOKEN_TILE, d), lambda i: (i, 0))
    in_specs = [tile, _resident((1, d)), _resident(wg.shape), _resident(wu.shape), _resident(wd.shape)]
    args = [x, g, wg, wu, wd]
    if final_norm:
        in_specs.append(_resident((1, d)))
        args.append(g_final)
    return pl.pallas_call(
        functools.partial(_ffn_kernel, final_norm=final_norm),
        grid=(n // TOKEN_TILE,),
        in_specs=in_specs,
        out_specs=tile,
        out_shape=jax.ShapeDtypeStruct((n, d), F32),
        compiler_params=_params("parallel"),
        name="ffn_final" if final_norm else "ffn",
    )(*args)


def _inproj_kernel(x_ref, g_ref, w_ref, q_ref, kv_ref, xr_ref, yr_ref, ga_ref, gr_ref):
    h = _rmsnorm(x_ref[...], g_ref[...]).astype(BF16)
    col = 0
    for ref in (q_ref, kv_ref, xr_ref, yr_ref, ga_ref, gr_ref):
        width = ref.shape[-1]
        ref[...] = _dot(h, w_ref[:, col:col + width]).astype(ref.dtype)
        col += width


def _inproj(x, g, w_in):
    n, d = x.shape
    widths = (Q_W, 2 * KV_W, D_RNN, D_RNN, D_MODEL, D_MODEL)
    dtypes = (BF16, BF16, F32, F32, F32, F32)
    return pl.pallas_call(
        _inproj_kernel,
        grid=(n // TOKEN_TILE,),
        in_specs=[pl.BlockSpec((TOKEN_TILE, d), lambda i: (i, 0)), _resident((1, d)), _resident(w_in.shape)],
        out_specs=[pl.BlockSpec((TOKEN_TILE, w), lambda i: (i, 0)) for w in widths],
        out_shape=[jax.ShapeDtypeStruct((n, w), t) for w, t in zip(widths, dtypes)],
        compiler_params=_params("parallel"),
        name="inproj",
    )(x, g, w_in)


def _attn_kernel(sink_ref, q_ref, kvp_ref, kvc_ref, kvn_ref, bias_ref, o_ref, *, seq_len):
    j = pl.program_id(1)
    kv = jnp.concatenate([kvp_ref[...], kvc_ref[...], kvn_ref[...]], axis=0)
    key_pos = j * Q_TILE - WINDOW + lax.broadcasted_iota(jnp.int32, (1, SPAN), 1)
    edge = jnp.where((key_pos >= 0) & (key_pos < seq_len), 0.0, NEG_INF).astype(F32)
    scale = HEAD_DIM ** -0.5
    for kh in range(N_KV_HEADS):
        k = kv[:, kh * HEAD_DIM:(kh + 1) * HEAD_DIM]
        v = kv[:, KV_W + kh * HEAD_DIM:KV_W + (kh + 1) * HEAD_DIM]
        for g in range(GROUP):
            h = kh * GROUP + g
            q = q_ref[:, h * HEAD_DIM:(h + 1) * HEAD_DIM]
            s = lax.dot_general(q, k, (((1,), (1,)), ((), ())), preferred_element_type=F32)
            s = s * scale + bias_ref[h] + edge
            sink = sink_ref[h]
            m = jnp.maximum(jnp.max(s, axis=-1, keepdims=True), sink)
            p = jnp.exp(s - m)
            denom = jnp.sum(p, axis=-1, keepdims=True) + jnp.exp(sink - m)
            o = _dot(p.astype(BF16), v) / denom
            o_ref[:, h * HEAD_DIM:(h + 1) * HEAD_DIM] = o.astype(o_ref.dtype)


def _attn_bias():
    qi = jnp.arange(Q_TILE)[:, None]
    kc = jnp.arange(SPAN)[None, :]
    dist = jnp.abs(qi + WINDOW - kc)
    slopes = 2.0 ** (-8.0 * jnp.arange(1, N_HEADS + 1, dtype=F32) / N_HEADS)
    bias = -slopes[:, None, None] * dist.astype(F32)[None]
    return jnp.where((dist <= WINDOW)[None], bias, NEG_INF).astype(F32)


def _attn(q, kv, sink, bias):
    b, s, _ = q.shape
    per_tile = Q_TILE // WINDOW
    last = s // WINDOW - 1
    return pl.pallas_call(
        functools.partial(_attn_kernel, seq_len=s),
        grid=(b, s // Q_TILE),
        in_specs=[
            pl.BlockSpec(memory_space=pltpu.SMEM),
            pl.BlockSpec((None, Q_TILE, Q_W), lambda i, j: (i, j, 0)),
            pl.BlockSpec((None, WINDOW, 2 * KV_W), lambda i, j: (i, jnp.maximum(j * per_tile - 1, 0), 0)),
            pl.BlockSpec((None, Q_TILE, 2 * KV_W), lambda i, j: (i, j, 0)),
            pl.BlockSpec((None, WINDOW, 2 * KV_W), lambda i, j: (i, jnp.minimum((j + 1) * per_tile, last), 0)),
            _resident(bias.shape),
        ],
        out_specs=pl.BlockSpec((None, Q_TILE, Q_W), lambda i, j: (i, j, 0)),
        out_shape=jax.ShapeDtypeStruct((b, s, Q_W), BF16),
        compiler_params=_params("parallel", "parallel"),
        name="attn",
    )(sink, q, kv, kv, kv, bias)


def _softplus(x):
    return jnp.maximum(x, 0.0) + jnp.log1p(jnp.exp(-jnp.abs(x)))


def _rnn_kernel(xp_ref, xc_ref, xn_ref, cw_ref, cb_ref, w_ref, ba_ref, bi_ref, lam_ref, *rest,
                n_tiles, reverse):
    if reverse:
        hf_ref, yr_ref, o_ref, a_s, u_s, h_s, carry = rest
    else:
        o_ref, a_s, u_s, carry = rest
        h_s = o_ref
    j = pl.program_id(1)
    tile = (n_tiles - 1 - j) if reverse else j
    t = RNN_TILE

    prev = jnp.where(tile > 0, xp_ref[...], 0.0)
    nxt = jnp.where(tile < n_tiles - 1, xn_ref[...], 0.0)
    ext = jnp.concatenate([prev, xc_ref[...], nxt], axis=0)
    xc = cb_ref[...] + ext[HALO - CONV_LEFT:HALO - CONV_LEFT + t] * cw_ref[0:1, :]
    for tap in range(1, CONV_WIDTH):
        lo = HALO - CONV_LEFT + tap
        xc = xc + ext[lo:lo + t] * cw_ref[tap:tap + 1, :]

    decay_rate = -LRU_C * _softplus(-lam_ref[...])
    for n in range(N_RNN_BLOCKS):
        cols = slice(n * RNN_BLOCK, (n + 1) * RNN_BLOCK)
        xb = xc[:, cols]
        pre = _dot(xb.astype(BF16), w_ref[n])
        r = jax.nn.sigmoid(pre[:, :RNN_BLOCK] + ba_ref[:, cols])
        ig = jax.nn.sigmoid(pre[:, RNN_BLOCK:] + bi_ref[:, cols])
        a = jnp.exp(r * decay_rate[:, cols])
        a_s[:, cols] = a
        u_s[:, cols] = jnp.sqrt(1.0 - a * a) * (ig * xb)

    @pl.when(j == 0)
    def _():
        carry[...] = jnp.zeros_like(carry)

    row = lax.broadcasted_iota(jnp.int32, (V7X_SUBLANES, D_RNN), 0)
    n_groups = t // V7X_SUBLANES

    def group(i, h_in):
        gi = (n_groups - 1 - i) if reverse else i
        rows = pl.ds(pl.multiple_of(gi * V7X_SUBLANES, V7X_SUBLANES), V7X_SUBLANES)
        a = a_s[rows, :]
        u = u_s[rows, :]
        for step in (1, 2, 4):
            if reverse:
                keep = row < V7X_SUBLANES - step
                shift = V7X_SUBLANES - step
            else:
                keep = row >= step
                shift = step
            a_sh = jnp.where(keep, pltpu.roll(a, shift, 0), 1.0)
            u_sh = jnp.where(keep, pltpu.roll(u, shift, 0), 0.0)
            u = a * u_sh + u
            a = a * a_sh
        h = a * h_in + u
        h_s[rows, :] = h
        return h[0:1, :] if reverse else h[V7X_SUBLANES - 1:V7X_SUBLANES, :]

    carry[...] = lax.fori_loop(0, n_groups, group, carry[...], unroll=2)

    if reverse:
        hr = hf_ref[...] + h_s[...]
        o_ref[...] = (hr * jax.nn.gelu(yr_ref[...])).astype(o_ref.dtype)


def _rnn(xr, conv_w, conv_b, w_gates, ba, bi, lam, h_fwd=None, yr=None):
    b, s, d = xr.shape
    reverse = h_fwd is not None
    n_tiles = s // RNN_TILE
    halos_per_tile = RNN_TILE // HALO
    last_halo = s // HALO - 1

    def seq_tile(j):
        return (n_tiles - 1 - j) if reverse else j

    tile = pl.BlockSpec((None, RNN_TILE, d), lambda i, j: (i, seq_tile(j), 0))
    in_specs = [
        pl.BlockSpec((None, HALO, d), lambda i, j: (i, jnp.maximum(seq_tile(j) * halos_per_tile - 1, 0), 0)),
        tile,
        pl.BlockSpec((None, HALO, d), lambda i, j: (i, jnp.minimum((seq_tile(j) + 1) * halos_per_tile, last_halo), 0)),
        _resident(conv_w.shape), _resident(conv_b.shape), _resident(w_gates.shape),
        _resident(ba.shape), _resident(bi.shape), _resident(lam.shape),
    ]
    args = [xr, xr, xr, conv_w, conv_b, w_gates, ba, bi, lam]
    scratch = [pltpu.VMEM((RNN_TILE, d), F32), pltpu.VMEM((RNN_TILE, d), F32)]
    if reverse:
        in_specs += [tile, tile]
        args += [h_fwd, yr]
        scratch.append(pltpu.VMEM((RNN_TILE, d), F32))
    scratch.append(pltpu.VMEM((1, d), F32))
    return pl.pallas_call(
        functools.partial(_rnn_kernel, n_tiles=n_tiles, reverse=reverse),
        grid=(b, n_tiles),
        in_specs=in_specs,
        out_specs=tile,
        out_shape=jax.ShapeDtypeStruct((b, s, d), BF16 if reverse else F32),
        scratch_shapes=scratch,
        compiler_params=_params("parallel", "arbitrary"),
        name="rnn_bwd" if reverse else "rnn_fwd",
    )(*args)


def _mix_kernel(x_ref, at_ref, rn_ref, ga_ref, gr_ref, wa_ref, wr_ref, wo_ref, o_ref):
    attn = _dot(at_ref[...], wa_ref[...])
    rnn = _dot(rn_ref[...], wr_ref[...])
    merged = jax.nn.sigmoid(ga_ref[...]) * attn + jax.nn.sigmoid(gr_ref[...]) * rnn
    o_ref[...] = x_ref[...] + _dot(merged.astype(BF16), wo_ref[...])


def _mix(x, attn, rnn, ga, gr, w_attn_o, w_rnn_o, w_out):
    n, d = x.shape
    tile = pl.BlockSpec((TOKEN_TILE, d), lambda i: (i, 0))
    return pl.pallas_call(
        _mix_kernel,
        grid=(n // TOKEN_TILE,),
        in_specs=[tile] * 5 + [_resident(w_attn_o.shape), _resident(w_rnn_o.shape), _resident(w_out.shape)],
        out_specs=tile,
        out_shape=jax.ShapeDtypeStruct((n, d), F32),
        compiler_params=_params("parallel"),
        name="mix",
    )(x, attn, rnn, ga, gr, w_attn_o, w_rnn_o, w_out)


def _layer(x, w, g_final):
    b, s, d = x.shape
    n = b * s
    x = x.reshape(n, d)
    x = _ffn(x, w["norm_ffn1"], w["ffn1_gate"], w["ffn1_up"], w["ffn1_down"])
    q, kv, xr, yr, ga, gr = _inproj(x, w["norm_mix"], w["w_in"])
    attn = _attn(q.reshape(b, s, -1), kv.reshape(b, s, -1), w["attn_sink"], w["attn_bias"])
    xr = xr.reshape(b, s, -1)
    h_fwd = _rnn(xr, w["conv_w"], w["conv_b"], w["gates_f"], w["ba_f"], w["bi_f"], w["lam_f"])
    rnn = _rnn(xr, w["conv_w"], w["conv_b"], w["gates_b"], w["ba_b"], w["bi_b"], w["lam_b"],
               h_fwd=h_fwd, yr=yr.reshape(b, s, -1))
    x = _mix(x, attn.reshape(n, -1), rnn.reshape(n, -1), ga, gr, w["w_attn_o"], w["w_rnn_o"], w["w_out"])
    x = _ffn(x, w["norm_ffn2"], w["ffn2_gate"], w["ffn2_up"], w["ffn2_down"], g_final=g_final)
    return x.reshape(b, s, d)


def kernel(x_prompt, x_sample, norm_ffn1, ffn1_gate, ffn1_up, ffn1_down, norm_mix, w_in, attn_sink, conv_w, conv_b, lru_wa_f, lru_ba_f, lru_wi_f, lru_bi_f, lru_lam_f, lru_wa_b, lru_ba_b, lru_wi_b, lru_bi_b, lru_lam_b, w_attn_o, w_rnn_o, w_out, norm_ffn2, ffn2_gate, ffn2_up, ffn2_down, norm_final):
    depth = norm_ffn1.shape[0]
    outs = [x_prompt, x_sample]
    for l in range(depth):
        w = {
            "norm_ffn1": norm_ffn1[l][None], "norm_mix": norm_mix[l][None], "norm_ffn2": norm_ffn2[l][None],
            "ffn1_gate": ffn1_gate[l].astype(BF16), "ffn1_up": ffn1_up[l].astype(BF16),
            "ffn1_down": ffn1_down[l].astype(BF16),
            "ffn2_gate": ffn2_gate[l].astype(BF16), "ffn2_up": ffn2_up[l].astype(BF16),
            "ffn2_down": ffn2_down[l].astype(BF16),
            "w_in": w_in[l].astype(BF16),
            "attn_sink": attn_sink[l], "attn_bias": _attn_bias(),
            "conv_w": conv_w[l], "conv_b": conv_b[l][None],
            "gates_f": jnp.concatenate([lru_wa_f[l], lru_wi_f[l]], axis=-1).astype(BF16),
            "gates_b": jnp.concatenate([lru_wa_b[l], lru_wi_b[l]], axis=-1).astype(BF16),
            "ba_f": lru_ba_f[l][None], "bi_f": lru_bi_f[l][None], "lam_f": lru_lam_f[l][None],
            "ba_b": lru_ba_b[l][None], "bi_b": lru_bi_b[l][None], "lam_b": lru_lam_b[l][None],
            "w_attn_o": w_attn_o[l].astype(BF16), "w_rnn_o": w_rnn_o[l].astype(BF16),
            "w_out": w_out[l].astype(BF16),
        }
        g_final = norm_final[None] if l == depth - 1 else None
        outs = [_layer(x, w, g_final) for x in outs]
    return tuple(outs)
```

```python
import functools
import math
from typing import Any, Callable, NamedTuple

import jax
import jax.numpy as jnp
from jax import lax
from jax.experimental import pallas as pl
from jax.experimental.pallas import tpu as pltpu

D_MODEL = 1024
N_HEADS = 8
N_KV_HEADS = 2
HEAD_DIM = 128
GROUP = N_HEADS // N_KV_HEADS
WINDOW = 128
Q_W = N_HEADS * HEAD_DIM
KV_W = N_KV_HEADS * HEAD_DIM
NEG_INF = -1e30
D_RNN = 1024
N_RNN_BLOCKS = 8
RNN_BLOCK = D_RNN // N_RNN_BLOCKS
CONV_WIDTH = 4
CONV_LEFT = 2
LRU_C = 8.0
D_FF = 2816
EPS = 1e-6
LOG2E = math.log2(math.e)

V7X_SUBLANES = 8
V7X_MXU_DIM = 256
V7X_VMEM_LIMIT_BYTES = 56 * 1024 * 1024

N_PARTS = 3
TILE = 512
FF_CHUNK = V7X_MXU_DIM
INPROJ_CHUNK = 2 * V7X_MXU_DIM
MIX_CHUNK = 2 * V7X_MXU_DIM
Q_SUB = 256
SPAN = Q_SUB + 2 * WINDOW
HALO = V7X_SUBLANES
SCAN_PARTS = 4

BF16 = jnp.bfloat16
F32 = jnp.float32


class Stage(NamedTuple):
    body: Callable[..., list]
    args: tuple
    in_specs: tuple
    out_specs: tuple
    out_shapes: tuple
    scratch: tuple = ()
    init: Any = None


def _run(name, n_steps, *stages):
    counts = [(len(s.args), len(s.out_shapes), len(s.scratch)) for s in stages]
    n_in = sum(c[0] for c in counts)
    n_out = sum(c[1] for c in counts)

    def kernel(*refs):
        step = pl.program_id(0)
        if any(s.init is not None for s in stages):
            @pl.when(step == 0)
            def _():
                s0 = n_in + n_out
                for stage, (_, _, ns) in zip(stages, counts):
                    if stage.init is not None:
                        stage.init(*refs[s0:s0 + ns])
                    s0 += ns
        i0, o0, s0 = 0, n_in, n_in + n_out
        slices = []
        for stage, (ni, no, ns) in zip(stages, counts):
            pieces = stage.body(step, *refs[i0:i0 + ni], *refs[o0:o0 + no], *refs[s0:s0 + ns])
            slices += [((k + 0.5) / len(pieces), piece) for k, piece in enumerate(pieces)]
            i0, o0, s0 = i0 + ni, o0 + no, s0 + ns
        for _, piece in sorted(slices, key=lambda item: item[0]):
            piece()

    outs = pl.pallas_call(
        kernel,
        grid=(n_steps,),
        in_specs=[spec for s in stages for spec in s.in_specs],
        out_specs=[spec for s in stages for spec in s.out_specs],
        out_shape=[shape for s in stages for shape in s.out_shapes],
        scratch_shapes=[scr for s in stages for scr in s.scratch],
        compiler_params=pltpu.CompilerParams(
            dimension_semantics=("arbitrary",), vmem_limit_bytes=V7X_VMEM_LIMIT_BYTES),
        name=name,
    )(*[a for s in stages for a in s.args])
    result, k = [], 0
    for _, no, _ in counts:
        result.append(outs[k:k + no])
        k += no
    return result


def _resident(shape):
    return pl.BlockSpec(shape, lambda i: (0,) * len(shape), pipeline_mode=pl.Buffered(1))


def _rows(width, first_tile=0):
    return pl.BlockSpec((TILE, width), lambda i: (first_tile + i, 0))


def _rmsnorm(x, g):
    y = x * lax.rsqrt(jnp.mean(x * x, axis=-1, keepdims=True) + EPS)
    return y * g


def _dot(a, b):
    return jnp.dot(a, b, preferred_element_type=F32)


def _ffn_body(step, x_ref, g_ref, wg_ref, wu_ref, wd_ref, *rest, final_norm, passthrough):
    del step
    rest = list(rest)
    gf_ref = rest.pop(0) if final_norm else None
    y_prev_ref = rest.pop(0) if passthrough else None
    (o_ref,) = rest
    v = {}

    def start():
        v["h"] = _rmsnorm(x_ref[...], g_ref[...]).astype(BF16)
        v["acc"] = jnp.zeros(x_ref.shape, F32)

    def expand(c):
        cols = slice(c * FF_CHUNK, (c + 1) * FF_CHUNK)
        v["gate"] = _dot(v["h"], wg_ref[:, cols])
        v["up"] = _dot(v["h"], wu_ref[:, cols])

    def contract(c):
        cols = slice(c * FF_CHUNK, (c + 1) * FF_CHUNK)
        act = (jax.nn.silu(v["gate"]) * v["up"]).astype(BF16)
        v["acc"] = v["acc"] + _dot(act, wd_ref[cols, :])

    def finish():
        y = x_ref[...] + 0.5 * v["acc"]
        if final_norm:
            y = _rmsnorm(y, gf_ref[...])
        if passthrough:
            o_ref[0] = y_prev_ref[...]
            o_ref[1] = y
        else:
            o_ref[...] = y

    pieces = [start]
    for c in range(D_FF // FF_CHUNK):
        pieces += [functools.partial(expand, c), functools.partial(contract, c)]
    return pieces + [finish]


def _ffn_stage(x, n_tokens, w, prefix, first_tile=0, g_final=None, y_prev=None):
    d = x.shape[-1]
    wg, wu, wd = w[prefix + "_gate"], w[prefix + "_up"], w[prefix + "_down"]
    args = [x, w["norm_" + prefix], wg, wu, wd]
    in_specs = [_rows(d, first_tile), _resident((1, d)), _resident(wg.shape), _resident(wu.shape),
                _resident(wd.shape)]
    if g_final is not None:
        args.append(g_final)
        in_specs.append(_resident((1, d)))
    if y_prev is not None:
        args.append(y_prev)
        in_specs.append(_rows(d))
        out_spec = pl.BlockSpec((2, TILE, d), lambda i: (0, i, 0))
        out_shape = jax.ShapeDtypeStruct((2, n_tokens, d), F32)
    else:
        out_spec = _rows(d)
        out_shape = jax.ShapeDtypeStruct((n_tokens, d), F32)
    body = functools.partial(_ffn_body, final_norm=g_final is not None, passthrough=y_prev is not None)
    return Stage(body, tuple(args), tuple(in_specs), (out_spec,), (out_shape,))


_INPROJ_WIDTHS = (Q_W, 2 * KV_W, D_RNN, D_RNN, D_MODEL, D_MODEL)
_INPROJ_DTYPES = (BF16, BF16, F32, BF16, BF16, BF16)


def _inproj_body(step, x_ref, g_ref, w_ref, q_ref, kv_ref, xr_ref, gy_ref, sga_ref, sgr_ref):
    del step
    v = {}

    def start():
        v["h"] = _rmsnorm(x_ref[...], g_ref[...]).astype(BF16)

    def project(ref, fn, col, lo, width):
        ref[:, lo:lo + width] = fn(_dot(v["h"], w_ref[:, col + lo:col + lo + width])).astype(ref.dtype)

    post = (
        lambda q: q * (HEAD_DIM ** -0.5 * LOG2E),
        lambda kv: kv,
        lambda xr: xr,
        jax.nn.gelu,
        jax.nn.sigmoid,
        jax.nn.sigmoid,
    )
    pieces = [start]
    col = 0
    for ref, fn in zip((q_ref, kv_ref, xr_ref, gy_ref, sga_ref, sgr_ref), post):
        for lo in range(0, ref.shape[-1], INPROJ_CHUNK):
            pieces.append(functools.partial(project, ref, fn, col, lo, INPROJ_CHUNK))
        col += ref.shape[-1]
    return pieces


def _inproj_stage(x, n_tokens, w):
    d = x.shape[-1]
    return Stage(
        _inproj_body, (x, w["norm_mix"], w["w_in"]),
        (_rows(d), _resident((1, d)), _resident(w["w_in"].shape)),
        tuple(_rows(width) for width in _INPROJ_WIDTHS),
        tuple(jax.ShapeDtypeStruct((n_tokens, width), t) for width, t in zip(_INPROJ_WIDTHS, _INPROJ_DTYPES)),
    )


def _attn_body(step, sink_ref, q_ref, kvp_ref, kvc_ref, kvn_ref, bias_ref, o_ref, *, seq_len):
    tiles_per_seq = seq_len // TILE
    j = step % tiles_per_seq
    n_sub = TILE // Q_SUB
    v = {}

    def start():
        v["kv"] = jnp.concatenate([kvp_ref[...], kvc_ref[...], kvn_ref[...]], axis=0)
        lane = lax.broadcasted_iota(jnp.int32, (1, WINDOW), 1)
        v["edge_lo"] = jnp.where(j * TILE - WINDOW + lane >= 0, 0.0, NEG_INF).astype(F32)
        v["edge_hi"] = jnp.where((j + 1) * TILE + lane < seq_len, 0.0, NEG_INF).astype(F32)

    def head(sub, h):
        kh = h // GROUP
        span = v["kv"][sub * Q_SUB:sub * Q_SUB + SPAN]
        k = span[:, kh * HEAD_DIM:(kh + 1) * HEAD_DIM]
        val = span[:, KV_W + kh * HEAD_DIM:KV_W + (kh + 1) * HEAD_DIM]
        cols = slice(h * HEAD_DIM, (h + 1) * HEAD_DIM)
        q = q_ref[sub * Q_SUB:(sub + 1) * Q_SUB, cols]
        s = lax.dot_general(q, k, (((1,), (1,)), ((), ())), preferred_element_type=F32)
        s = s + bias_ref[h]
        if sub == 0:
            s = jnp.concatenate([s[:, :WINDOW] + v["edge_lo"], s[:, WINDOW:]], axis=1)
        if sub == n_sub - 1:
            s = jnp.concatenate([s[:, :SPAN - WINDOW], s[:, SPAN - WINDOW:] + v["edge_hi"]], axis=1)
        sink = sink_ref[h] * LOG2E
        m = jnp.maximum(jnp.max(s, axis=-1, keepdims=True), sink)
        p = jnp.exp2(s - m)
        denom = jnp.sum(p, axis=-1, keepdims=True) + jnp.exp2(sink - m)
        o = _dot(p.astype(BF16), val) / denom
        o_ref[sub * Q_SUB:(sub + 1) * Q_SUB, cols] = o.astype(o_ref.dtype)

    return [start] + [functools.partial(head, sub, h) for sub in range(n_sub) for h in range(N_HEADS)]


def _attn_bias():
    qi = jnp.arange(Q_SUB)[:, None]
    kc = jnp.arange(SPAN)[None, :]
    dist = jnp.abs(qi + WINDOW - kc)
    slopes = 2.0 ** (-8.0 * jnp.arange(1, N_HEADS + 1, dtype=F32) / N_HEADS)
    bias = -slopes[:, None, None] * dist.astype(F32)[None] * LOG2E
    return jnp.where((dist <= WINDOW)[None], bias, NEG_INF).astype(F32)


def _attn_stage(q, kv, batch, seq_len, w):
    tiles_per_seq = seq_len // TILE
    halos_per_tile = TILE // WINDOW
    last_halo = seq_len // WINDOW - 1

    def bj(i):
        return i // tiles_per_seq, i % tiles_per_seq

    def tile(width):
        return pl.BlockSpec((None, TILE, width), lambda i: (*bj(i), 0))

    def halo(offset):
        def index(i):
            b, j = bj(i)
            return b, jnp.clip(j * halos_per_tile + offset, 0, last_halo), 0
        return pl.BlockSpec((None, WINDOW, 2 * KV_W), index)

    q = q.reshape(batch, seq_len, Q_W)
    kv = kv.reshape(batch, seq_len, 2 * KV_W)
    bias = w["attn_bias"]
    return Stage(
        functools.partial(_attn_body, seq_len=seq_len),
        (w["attn_sink"], q, kv, kv, kv, bias),
        (pl.BlockSpec(memory_space=pltpu.SMEM), tile(Q_W), halo(-1), tile(2 * KV_W), halo(halos_per_tile),
         _resident(bias.shape)),
        (tile(Q_W),),
        (jax.ShapeDtypeStruct((batch, seq_len, Q_W), BF16),),
    )


def _softplus(x):
    return jnp.maximum(x, 0.0) + jnp.log1p(jnp.exp(-jnp.abs(x)))


def _rnn_body(step, xp_ref, xc_ref, xn_ref, cw_ref, cb_ref, w_ref, ba_ref, bi_ref, lam_ref, *rest,
              seq_len, reverse):
    if reverse:
        hf_ref, gy_ref, o_ref, carry = rest
    else:
        o_ref, carry = rest
    n_tiles = seq_len // TILE
    j = step % n_tiles
    tile = (n_tiles - 1 - j) if reverse else j
    n_groups = TILE // V7X_SUBLANES
    row = lax.broadcasted_iota(jnp.int32, (V7X_SUBLANES, RNN_BLOCK), 0)

    v = {}

    def gates(n):
        cols = slice(n * RNN_BLOCK, (n + 1) * RNN_BLOCK)
        prev = jnp.where(tile > 0, xp_ref[:, cols], 0.0)
        nxt = jnp.where(tile < n_tiles - 1, xn_ref[:, cols], 0.0)
        ext = jnp.concatenate([prev, xc_ref[:, cols], nxt], axis=0)
        xb = cb_ref[:, cols] + ext[HALO - CONV_LEFT:HALO - CONV_LEFT + TILE] * cw_ref[0:1, cols]
        for tap in range(1, CONV_WIDTH):
            lo = HALO - CONV_LEFT + tap
            xb = xb + ext[lo:lo + TILE] * cw_ref[tap:tap + 1, cols]

        decay_rate = -LRU_C * _softplus(-lam_ref[:, cols])
        pre = _dot(xb.astype(BF16), w_ref[n])
        r = jax.nn.sigmoid(pre[:, :RNN_BLOCK] + ba_ref[:, cols])
        ig = jax.nn.sigmoid(pre[:, RNN_BLOCK:] + bi_ref[:, cols])
        a_all = jnp.exp(r * decay_rate)
        v["a"] = a_all
        v["u"] = jnp.sqrt(1.0 - a_all * a_all) * (ig * xb)
        v["h"] = jnp.where(j == 0, 0.0, carry[:, cols])
        v["out"] = [None] * n_groups

    def scan(n, part):
        cols = slice(n * RNN_BLOCK, (n + 1) * RNN_BLOCK)
        per_part = n_groups // SCAN_PARTS
        h, out = v["h"], v["out"]
        for i in range(part * per_part, (part + 1) * per_part):
            gi = (n_groups - 1 - i) if reverse else i
            a = v["a"][gi * V7X_SUBLANES:(gi + 1) * V7X_SUBLANES]
            u = v["u"][gi * V7X_SUBLANES:(gi + 1) * V7X_SUBLANES]
            for shift in (1, 2, 4):
                if reverse:
                    keep = row < V7X_SUBLANES - shift
                    amount = V7X_SUBLANES - shift
                else:
                    keep = row >= shift
                    amount = shift
                a_sh = jnp.where(keep, pltpu.roll(a, amount, 0), 1.0)
                u_sh = jnp.where(keep, pltpu.roll(u, amount, 0), 0.0)
                u = a * u_sh + u
                a = a * a_sh
            hg = a * h + u
            out[gi] = hg
            h = hg[0:1, :] if reverse else hg[V7X_SUBLANES - 1:V7X_SUBLANES, :]
        v["h"] = h
        if part < SCAN_PARTS - 1:
            return
        carry[:, cols] = h
        h_all = jnp.concatenate(out, axis=0)
        if reverse:
            h_all = (hf_ref[:, cols] + h_all) * gy_ref[:, cols].astype(F32)
        o_ref[:, cols] = h_all.astype(o_ref.dtype)

    pieces = []
    for n in range(N_RNN_BLOCKS):
        pieces.append(functools.partial(gates, n))
        pieces += [functools.partial(scan, n, part) for part in range(SCAN_PARTS)]
    return pieces


def _rnn_init(*scratch):
    carry = scratch[-1]
    carry[...] = jnp.zeros_like(carry)


def _rnn_stage(xr, batch, seq_len, w, direction, h_fwd=None, gy=None):
    d = D_RNN
    reverse = h_fwd is not None
    n_tiles = seq_len // TILE
    halos_per_tile = TILE // HALO
    last_halo = seq_len // HALO - 1

    def bt(i):
        j = i % n_tiles
        return i // n_tiles, (n_tiles - 1 - j) if reverse else j

    tile = pl.BlockSpec((None, TILE, d), lambda i: (*bt(i), 0))

    def halo(offset):
        def index(i):
            b, t = bt(i)
            return b, jnp.clip(t * halos_per_tile + offset, 0, last_halo), 0
        return pl.BlockSpec((None, HALO, d), index)

    xr = xr.reshape(batch, seq_len, d)
    params = [w["conv_w"], w["conv_b"]] + [w[k + direction] for k in ("gates_", "ba_", "bi_", "lam_")]
    args = [xr, xr, xr] + params
    in_specs = [halo(-1), tile, halo(halos_per_tile)] + [_resident(p.shape) for p in params]
    if reverse:
        args += [h_fwd, gy.reshape(batch, seq_len, d)]
        in_specs += [tile, tile]
    scratch = [pltpu.VMEM((1, d), F32)]
    return Stage(
        functools.partial(_rnn_body, seq_len=seq_len, reverse=reverse),
        tuple(args), tuple(in_specs), (tile,),
        (jax.ShapeDtypeStruct((batch, seq_len, d), BF16 if reverse else F32),),
        tuple(scratch),
        init=_rnn_init,
    )


def _mix_body(step, x_ref, at_ref, rn_ref, sga_ref, sgr_ref, wa_ref, wr_ref, wo_ref, o_ref):
    del step
    n_chunks = D_MODEL // MIX_CHUNK
    merged = [None] * n_chunks

    def merge(c):
        cols = slice(c * MIX_CHUNK, (c + 1) * MIX_CHUNK)
        attn = _dot(at_ref[...], wa_ref[:, cols])
        rnn = _dot(rn_ref[...], wr_ref[:, cols])
        merged[c] = (sga_ref[:, cols].astype(F32) * attn + sgr_ref[:, cols].astype(F32) * rnn).astype(BF16)

    def project(c):
        cols = slice(c * MIX_CHUNK, (c + 1) * MIX_CHUNK)
        if c == 0:
            merged[:] = [jnp.concatenate(merged, axis=1)]
        o_ref[:, cols] = x_ref[:, cols] + _dot(merged[0], wo_ref[:, cols])

    return ([functools.partial(merge, c) for c in range(n_chunks)]
            + [functools.partial(project, c) for c in range(n_chunks)])


def _mix_stage(x, attn, rnn, sga, sgr, n_tokens, w):
    d = x.shape[-1]
    weights = (w["w_attn_o"], w["w_rnn_o"], w["w_out"])
    return Stage(
        _mix_body,
        (x, attn.reshape(n_tokens, -1), rnn.reshape(n_tokens, -1), sga, sgr) + weights,
        (_rows(d),) * 5 + tuple(_resident(m.shape) for m in weights),
        (_rows(d),),
        (jax.ShapeDtypeStruct((n_tokens, d), F32),),
    )


def _layer(x_prompt, x_sample, w, g_final):
    d = x_prompt.shape[-1]
    n_tokens = x_prompt.shape[0] * x_prompt.shape[1]
    half_batch = x_sample.shape[0] // 2
    assert half_batch * x_sample.shape[1] == n_tokens and n_tokens % TILE == 0
    n_steps = n_tokens // TILE
    parts = (
        (x_prompt.reshape(-1, d), 0, x_prompt.shape[0], x_prompt.shape[1]),
        (x_sample.reshape(-1, d), 0, half_batch, x_sample.shape[1]),
        (x_sample.reshape(-1, d), n_steps, half_batch, x_sample.shape[1]),
    )
    st = [dict() for _ in range(N_PARTS)]

    def ffn1(p):
        x, first, _, _ = parts[p]
        return _ffn_stage(x, n_tokens, w, "ffn1", first_tile=first)

    def inproj(p):
        return _inproj_stage(st[p]["x1"], n_tokens, w)

    def attn(p):
        return _attn_stage(st[p]["q"], st[p]["kv"], parts[p][2], parts[p][3], w)

    def rnn_f(p):
        return _rnn_stage(st[p]["xr"], parts[p][2], parts[p][3], w, "f")

    def rnn_b(p):
        return _rnn_stage(st[p]["xr"], parts[p][2], parts[p][3], w, "b", h_fwd=st[p]["hf"], gy=st[p]["gy"])

    def mix(p):
        s = st[p]
        return _mix_stage(s["x1"], s["attn"], s["rnn"], s["sga"], s["sgr"], n_tokens, w)

    def ffn2(p, y_prev=None):
        return _ffn_stage(st[p]["x2"], n_tokens, w, "ffn2", g_final=g_final, y_prev=y_prev)

    produces = {ffn1: ("x1",), inproj: ("q", "kv", "xr", "gy", "sga", "sgr"), attn: ("attn",), rnn_f: ("hf",),
                rnn_b: ("rnn",), mix: ("x2",), ffn2: ("y",)}

    def call(name, *work, **kw):
        stages = [fn(p, **kw) if fn is ffn2 else fn(p) for fn, p in work]
        for (fn, p), outs in zip(work, _run(name, n_steps, *stages)):
            st[p].update(zip(produces[fn], outs))

    call("ffn1_p0", (ffn1, 0))
    call("inproj_p0", (inproj, 0))
    call("ffn1_p1__attn_p0", (ffn1, 1), (attn, 0))
    call("inproj_p1__rnnf_p0", (inproj, 1), (rnn_f, 0))
    call("ffn1_p2__rnnb_p0", (ffn1, 2), (rnn_b, 0))
    call("inproj_p2__attn_p1", (inproj, 2), (attn, 1))
    call("mix_p0__rnnf_p1", (mix, 0), (rnn_f, 1))
    call("ffn2_p0__rnnb_p1", (ffn2, 0), (rnn_b, 1))
    call("mix_p1__attn_p2", (mix, 1), (attn, 2))
    call("ffn2_p1__rnnf_p2", (ffn2, 1), (rnn_f, 2))
    call("rnnb_p2", (rnn_b, 2))
    call("mix_p2", (mix, 2))
    call("ffn2_p2", (ffn2, 2), y_prev=st[1]["y"])
    return st[0]["y"].reshape(x_prompt.shape), st[2]["y"].reshape(x_sample.shape)


def kernel(x_prompt, x_sample, norm_ffn1, ffn1_gate, ffn1_up, ffn1_down, norm_mix, w_in, attn_sink, conv_w, conv_b, lru_wa_f, lru_ba_f, lru_wi_f, lru_bi_f, lru_lam_f, lru_wa_b, lru_ba_b, lru_wi_b, lru_bi_b, lru_lam_b, w_attn_o, w_rnn_o, w_out, norm_ffn2, ffn2_gate, ffn2_up, ffn2_down, norm_final):
    depth = norm_ffn1.shape[0]
    for l in range(depth):
        w = {
            "norm_ffn1": norm_ffn1[l][None], "norm_mix": norm_mix[l][None], "norm_ffn2": norm_ffn2[l][None],
            "ffn1_gate": ffn1_gate[l].astype(BF16), "ffn1_up": ffn1_up[l].astype(BF16),
            "ffn1_down": ffn1_down[l].astype(BF16),
            "ffn2_gate": ffn2_gate[l].astype(BF16), "ffn2_up": ffn2_up[l].astype(BF16),
            "ffn2_down": ffn2_down[l].astype(BF16),
            "w_in": w_in[l].astype(BF16),
            "attn_sink": attn_sink[l], "attn_bias": _attn_bias(),
            "conv_w": conv_w[l], "conv_b": conv_b[l][None],
            "gates_f": jnp.concatenate([lru_wa_f[l], lru_wi_f[l]], axis=-1).astype(BF16),
            "gates_b": jnp.concatenate([lru_wa_b[l], lru_wi_b[l]], axis=-1).astype(BF16),
            "ba_f": lru_ba_f[l][None], "bi_f": lru_bi_f[l][None], "lam_f": lru_lam_f[l][None],
            "ba_b": lru_ba_b[l][None], "bi_b": lru_bi_b[l][None], "lam_b": lru_lam_b[l][None],
            "w_attn_o": w_attn_o[l].astype(BF16), "w_rnn_o": w_rnn_o[l].astype(BF16),
            "w_out": w_out[l].astype(BF16),
        }
        g_final = norm_final[None] if l == depth - 1 else None
        x_prompt, x_sample = _layer(x_prompt, x_sample, w, g_final)
    return (x_prompt, x_sample)
```

```python
import functools
import math
from typing import Any, Callable, NamedTuple

import jax
import jax.numpy as jnp
from jax import lax
from jax.experimental import pallas as pl
from jax.experimental.pallas import tpu as pltpu

D_MODEL = 1024
N_HEADS = 8
N_KV_HEADS = 2
HEAD_DIM = 128
GROUP = N_HEADS // N_KV_HEADS
WINDOW = 128
Q_W = N_HEADS * HEAD_DIM
KV_W = N_KV_HEADS * HEAD_DIM
NEG_INF = -1e30
D_RNN = 1024
N_RNN_BLOCKS = 8
RNN_BLOCK = D_RNN // N_RNN_BLOCKS
CONV_WIDTH = 4
CONV_LEFT = 2
LRU_C = 8.0
D_FF = 2816
EPS = 1e-6
LOG2E = math.log2(math.e)

V7X_SUBLANES = 8
V7X_MXU_DIM = 256
V7X_VMEM_LIMIT_BYTES = 56 * 1024 * 1024

N_PARTS = 3
TILE = 512
FF_CHUNK = V7X_MXU_DIM
INPROJ_CHUNK = 2 * V7X_MXU_DIM
MIX_CHUNK = 2 * V7X_MXU_DIM
FFN_OUT_CHUNK = V7X_MXU_DIM
Q_SUB = 256
SPAN = Q_SUB + 2 * WINDOW
HALO = V7X_SUBLANES
SCAN_PARTS = 4
N_SEGMENTS = V7X_SUBLANES
SEG_LEN = TILE // N_SEGMENTS
SEG_PITCH = SEG_LEN + 8
RNN_BUFFERS = 2
SQRT_FLOOR = 1e-30

BF16 = jnp.bfloat16
F32 = jnp.float32


class Stage(NamedTuple):
    body: Callable[..., list]
    args: tuple
    in_specs: tuple
    out_specs: tuple
    out_shapes: tuple
    scratch: tuple = ()
    init: Any = None


def _run(name, n_steps, *stages):
    counts = [(len(s.args), len(s.out_shapes), len(s.scratch)) for s in stages]
    n_in = sum(c[0] for c in counts)
    n_out = sum(c[1] for c in counts)

    def kernel(*refs):
        step = pl.program_id(0)
        if any(s.init is not None for s in stages):
            @pl.when(step == 0)
            def _():
                s0 = n_in + n_out
                for stage, (_, _, ns) in zip(stages, counts):
                    if stage.init is not None:
                        stage.init(*refs[s0:s0 + ns])
                    s0 += ns
        i0, o0, s0 = 0, n_in, n_in + n_out
        slices = []
        for stage, (ni, no, ns) in zip(stages, counts):
            pieces = stage.body(step, *refs[i0:i0 + ni], *refs[o0:o0 + no], *refs[s0:s0 + ns])
            slices += [((k + 0.5) / len(pieces), piece) for k, piece in enumerate(pieces)]
            i0, o0, s0 = i0 + ni, o0 + no, s0 + ns
        for _, piece in sorted(slices, key=lambda item: item[0]):
            piece()

    outs = pl.pallas_call(
        kernel,
        grid=(n_steps,),
        in_specs=[spec for s in stages for spec in s.in_specs],
        out_specs=[spec for s in stages for spec in s.out_specs],
        out_shape=[shape for s in stages for shape in s.out_shapes],
        scratch_shapes=[scr for s in stages for scr in s.scratch],
        compiler_params=pltpu.CompilerParams(
            dimension_semantics=("arbitrary",), vmem_limit_bytes=V7X_VMEM_LIMIT_BYTES),
        name=name,
    )(*[a for s in stages for a in s.args])
    result, k = [], 0
    for _, no, _ in counts:
        result.append(outs[k:k + no])
        k += no
    return result


def _resident(shape):
    return pl.BlockSpec(shape, lambda i: (0,) * len(shape), pipeline_mode=pl.Buffered(1))


def _rows(width, first_tile=0):
    return pl.BlockSpec((TILE, width), lambda i: (first_tile + i, 0))


def _rmsnorm(x, g):
    y = x * lax.rsqrt(jnp.mean(x * x, axis=-1, keepdims=True) + EPS)
    return y * g


def _dot(a, b):
    return jnp.dot(a, b, preferred_element_type=F32)


def _ffn_body(step, x_ref, g_ref, wg_ref, wu_ref, wd_ref, *rest, final_norm, passthrough):
    del step
    rest = list(rest)
    gf_ref = rest.pop(0) if final_norm else None
    y_prev_ref = rest.pop(0) if passthrough else None
    o_ref, h_s, act_s = rest
    y_ref = o_ref.at[1] if passthrough else o_ref

    def start():
        h_s[...] = _rmsnorm(x_ref[...], g_ref[...]).astype(BF16)

    def expand(c):
        cols = slice(c * FF_CHUNK, (c + 1) * FF_CHUNK)
        gate = _dot(h_s[...], wg_ref[:, cols])
        up = _dot(h_s[...], wu_ref[:, cols])
        act_s[:, cols] = (jax.nn.silu(gate) * up).astype(BF16)

    def contract(c):
        cols = slice(c * FFN_OUT_CHUNK, (c + 1) * FFN_OUT_CHUNK)
        y_ref[:, cols] = x_ref[:, cols] + 0.5 * _dot(act_s[...], wd_ref[:, cols])

    def finish():
        if final_norm:
            y_ref[...] = _rmsnorm(y_ref[...], gf_ref[...])
        if passthrough:
            o_ref[0] = y_prev_ref[...]

    pieces = [start] + [functools.partial(expand, c) for c in range(D_FF // FF_CHUNK)]
    pieces += [functools.partial(contract, c) for c in range(D_MODEL // FFN_OUT_CHUNK)]
    return pieces + [finish]


def _ffn_stage(x, n_tokens, w, prefix, first_tile=0, g_final=None, y_prev=None):
    d = x.shape[-1]
    wg, wu, wd = w[prefix + "_gate"], w[prefix + "_up"], w[prefix + "_down"]
    args = [x, w["norm_" + prefix], wg, wu, wd]
    in_specs = [_rows(d, first_tile), _resident((1, d)), _resident(wg.shape), _resident(wu.shape),
                _resident(wd.shape)]
    if g_final is not None:
        args.append(g_final)
        in_specs.append(_resident((1, d)))
    if y_prev is not None:
        args.append(y_prev)
        in_specs.append(_rows(d))
        out_spec = pl.BlockSpec((2, TILE, d), lambda i: (0, i, 0))
        out_shape = jax.ShapeDtypeStruct((2, n_tokens, d), F32)
    else:
        out_spec = _rows(d)
        out_shape = jax.ShapeDtypeStruct((n_tokens, d), F32)
    body = functools.partial(_ffn_body, final_norm=g_final is not None, passthrough=y_prev is not None)
    scratch = (pltpu.VMEM((TILE, d), BF16), pltpu.VMEM((TILE, D_FF), BF16))
    return Stage(body, tuple(args), tuple(in_specs), (out_spec,), (out_shape,), scratch)


_INPROJ_WIDTHS = (Q_W, 2 * KV_W, D_RNN, D_RNN, D_MODEL, D_MODEL)
_INPROJ_DTYPES = (BF16, BF16, F32, BF16, BF16, BF16)


def _inproj_body(step, x_ref, g_ref, w_ref, q_ref, kv_ref, xr_ref, gy_ref, sga_ref, sgr_ref, h_s):
    del step

    def start():
        h_s[...] = _rmsnorm(x_ref[...], g_ref[...]).astype(BF16)

    def project(ref, fn, col, lo, width):
        ref[:, lo:lo + width] = fn(_dot(h_s[...], w_ref[:, col + lo:col + lo + width])).astype(ref.dtype)

    post = (
        lambda q: q * (HEAD_DIM ** -0.5 * LOG2E),
        lambda kv: kv,
        lambda xr: xr,
        jax.nn.gelu,
        jax.nn.sigmoid,
        jax.nn.sigmoid,
    )
    pieces = [start]
    col = 0
    for ref, fn in zip((q_ref, kv_ref, xr_ref, gy_ref, sga_ref, sgr_ref), post):
        for lo in range(0, ref.shape[-1], INPROJ_CHUNK):
            pieces.append(functools.partial(project, ref, fn, col, lo, INPROJ_CHUNK))
        col += ref.shape[-1]
    return pieces


def _inproj_stage(x, n_tokens, w):
    d = x.shape[-1]
    return Stage(
        _inproj_body, (x, w["norm_mix"], w["w_in"]),
        (_rows(d), _resident((1, d)), _resident(w["w_in"].shape)),
        tuple(_rows(width) for width in _INPROJ_WIDTHS),
        tuple(jax.ShapeDtypeStruct((n_tokens, width), t) for width, t in zip(_INPROJ_WIDTHS, _INPROJ_DTYPES)),
        (pltpu.VMEM((TILE, d), BF16),),
    )


def _attn_body(step, sink_ref, q_ref, kvp_ref, kvc_ref, kvn_ref, bias_ref, o_ref, *, seq_len):
    tiles_per_seq = seq_len // TILE
    j = step % tiles_per_seq
    n_sub = TILE // Q_SUB
    v = {}

    def start():
        v["kv"] = jnp.concatenate([kvp_ref[...], kvc_ref[...], kvn_ref[...]], axis=0)
        lane = lax.broadcasted_iota(jnp.int32, (1, WINDOW), 1)
        v["edge_lo"] = jnp.where(j * TILE - WINDOW + lane >= 0, 0.0, NEG_INF).astype(F32)
        v["edge_hi"] = jnp.where((j + 1) * TILE + lane < seq_len, 0.0, NEG_INF).astype(F32)

    def group(sub, kh):
        rows = slice(sub * Q_SUB, (sub + 1) * Q_SUB)
        span = v["kv"][sub * Q_SUB:sub * Q_SUB + SPAN]
        k = span[:, kh * HEAD_DIM:(kh + 1) * HEAD_DIM]
        val = span[:, KV_W + kh * HEAD_DIM:KV_W + (kh + 1) * HEAD_DIM]
        heads = range(kh * GROUP, (kh + 1) * GROUP)
        q = jnp.concatenate([q_ref[rows, h * HEAD_DIM:(h + 1) * HEAD_DIM] for h in heads], axis=0)
        s_all = lax.dot_general(q, k, (((1,), (1,)), ((), ())), preferred_element_type=F32)
        probs, denoms = [], []
        for g, h in enumerate(heads):
            s = s_all[g * Q_SUB:(g + 1) * Q_SUB] + bias_ref[h]
            if sub == 0:
                s = jnp.concatenate([s[:, :WINDOW] + v["edge_lo"], s[:, WINDOW:]], axis=1)
            if sub == n_sub - 1:
                s = jnp.concatenate([s[:, :SPAN - WINDOW], s[:, SPAN - WINDOW:] + v["edge_hi"]], axis=1)
            sink = sink_ref[h] * LOG2E
            m = jnp.maximum(jnp.max(s, axis=-1, keepdims=True), sink)
            p = jnp.exp2(s - m)
            denoms.append(jnp.sum(p, axis=-1, keepdims=True) + jnp.exp2(sink - m))
            probs.append(p.astype(BF16))
        o_all = _dot(jnp.concatenate(probs, axis=0), val)
        for g, h in enumerate(heads):
            o = o_all[g * Q_SUB:(g + 1) * Q_SUB] / denoms[g]
            o_ref[rows, h * HEAD_DIM:(h + 1) * HEAD_DIM] = o.astype(o_ref.dtype)

    return [start] + [functools.partial(group, sub, kh) for sub in range(n_sub) for kh in range(N_KV_HEADS)]


def _attn_bias():
    qi = jnp.arange(Q_SUB)[:, None]
    kc = jnp.arange(SPAN)[None, :]
    dist = jnp.abs(qi + WINDOW - kc)
    slopes = 2.0 ** (-8.0 * jnp.arange(1, N_HEADS + 1, dtype=F32) / N_HEADS)
    bias = -slopes[:, None, None] * dist.astype(F32)[None] * LOG2E
    return jnp.where((dist <= WINDOW)[None], bias, NEG_INF).astype(F32)


def _attn_stage(q, kv, batch, seq_len, w):
    tiles_per_seq = seq_len // TILE
    halos_per_tile = TILE // WINDOW
    last_halo = seq_len // WINDOW - 1

    def bj(i):
        return i // tiles_per_seq, i % tiles_per_seq

    def tile(width):
        return pl.BlockSpec((None, TILE, width), lambda i: (*bj(i), 0))

    def halo(offset):
        def index(i):
            b, j = bj(i)
            return b, jnp.clip(j * halos_per_tile + offset, 0, last_halo), 0
        return pl.BlockSpec((None, WINDOW, 2 * KV_W), index)

    q = q.reshape(batch, seq_len, Q_W)
    kv = kv.reshape(batch, seq_len, 2 * KV_W)
    bias = w["attn_bias"]
    return Stage(
        functools.partial(_attn_body, seq_len=seq_len),
        (w["attn_sink"], q, kv, kv, kv, bias),
        (pl.BlockSpec(memory_space=pltpu.SMEM), tile(Q_W), halo(-1), tile(2 * KV_W), halo(halos_per_tile),
         _resident(bias.shape)),
        (tile(Q_W),),
        (jax.ShapeDtypeStruct((batch, seq_len, Q_W), BF16),),
    )


def _softplus(x):
    return jnp.maximum(x, 0.0) + jnp.log1p(jnp.exp(-jnp.abs(x)))


def _sigmoid(x):
    return 0.5 * jnp.tanh(0.5 * x) + 0.5


def _sqrt_nonneg(y):
    return y * lax.rsqrt(jnp.maximum(y, SQRT_FLOOR))


def _rnn_body(step, xp_ref, xc_ref, xn_ref, cw_ref, cb_ref, w_ref, ba_ref, bi_ref, lam_ref, *rest,
              seq_len, reverse):
    if reverse:
        hf_ref, gy_ref, o_ref, ext_s, a_s, u_s, h_s, p_s, carry = rest
    else:
        o_ref, ext_s, a_s, u_s, h_s, p_s, carry = rest
    n_tiles = seq_len // TILE
    j = step % n_tiles
    tile = (n_tiles - 1 - j) if reverse else j
    v = {}

    def seg_rows(s):
        return slice(s * SEG_PITCH, s * SEG_PITCH + SEG_LEN)

    def gates(n):
        cols = slice(n * RNN_BLOCK, (n + 1) * RNN_BLOCK)
        buf = n % RNN_BUFFERS
        ext_s[buf, 0:HALO] = jnp.where(tile > 0, xp_ref[:, cols], 0.0)
        ext_s[buf, HALO:HALO + TILE] = xc_ref[:, cols]
        ext_s[buf, HALO + TILE:] = jnp.where(tile < n_tiles - 1, xn_ref[:, cols], 0.0)
        xb = cb_ref[:, cols]
        for tap in range(CONV_WIDTH):
            lo = HALO - CONV_LEFT + tap
            xb = xb + ext_s[buf, lo:lo + TILE] * cw_ref[tap:tap + 1, cols]

        decay_rate = (-LRU_C * LOG2E) * _softplus(-lam_ref[:, cols])
        pre = _dot(xb.astype(BF16), w_ref[n])
        r = _sigmoid(pre[:, :RNN_BLOCK] + ba_ref[:, cols])
        ig = _sigmoid(pre[:, RNN_BLOCK:] + bi_ref[:, cols])
        a = jnp.exp2(r * decay_rate)
        u = _sqrt_nonneg(1.0 - a * a) * (ig * xb)
        for s in range(N_SEGMENTS):
            a_s[buf, seg_rows(s)] = a[s * SEG_LEN:(s + 1) * SEG_LEN]
            u_s[buf, seg_rows(s)] = u[s * SEG_LEN:(s + 1) * SEG_LEN]
        v["h"] = jnp.zeros((N_SEGMENTS, RNN_BLOCK), F32)
        v["p"] = jnp.ones((N_SEGMENTS, RNN_BLOCK), F32)

    def scan(n, part):
        cols = slice(n * RNN_BLOCK, (n + 1) * RNN_BLOCK)
        buf = n % RNN_BUFFERS
        per_part = SEG_LEN // SCAN_PARTS
        h, p = v["h"], v["p"]
        for i in range(part * per_part, (part + 1) * per_part):
            k = (SEG_LEN - 1 - i) if reverse else i
            rows = pl.ds(k, N_SEGMENTS, stride=SEG_PITCH)
            a = a_s[buf, rows]
            h = a * h + u_s[buf, rows]
            p = a * p
            h_s[buf, rows] = h
            p_s[buf, rows] = p
        v["h"], v["p"] = h, p
        if part < SCAN_PARTS - 1:
            return
        state = jnp.where(j == 0, 0.0, carry[:, cols])
        order = range(N_SEGMENTS - 1, -1, -1) if reverse else range(N_SEGMENTS)
        entering = [None] * N_SEGMENTS
        for s in order:
            entering[s] = state
            state = p[s:s + 1] * state + h[s:s + 1]
        carry[:, cols] = state
        for s in range(N_SEGMENTS):
            out_rows = slice(s * SEG_LEN, (s + 1) * SEG_LEN)
            h_seg = h_s[buf, seg_rows(s)] + p_s[buf, seg_rows(s)] * entering[s]
            if reverse:
                h_seg = (hf_ref[out_rows, cols] + h_seg) * gy_ref[out_rows, cols].astype(F32)
            o_ref[out_rows, cols] = h_seg.astype(o_ref.dtype)

    pieces = []
    for n in range(N_RNN_BLOCKS):
        pieces.append(functools.partial(gates, n))
        pieces += [functools.partial(scan, n, part) for part in range(SCAN_PARTS)]
    return pieces


def _rnn_init(*scratch):
    carry = scratch[-1]
    carry[...] = jnp.zeros_like(carry)


def _rnn_stage(xr, batch, seq_len, w, direction, h_fwd=None, gy=None):
    d = D_RNN
    reverse = h_fwd is not None
    n_tiles = seq_len // TILE
    halos_per_tile = TILE // HALO
    last_halo = seq_len // HALO - 1

    def bt(i):
        j = i % n_tiles
        return i // n_tiles, (n_tiles - 1 - j) if reverse else j

    tile = pl.BlockSpec((None, TILE, d), lambda i: (*bt(i), 0))

    def halo(offset):
        def index(i):
            b, t = bt(i)
            return b, jnp.clip(t * halos_per_tile + offset, 0, last_halo), 0
        return pl.BlockSpec((None, HALO, d), index)

    xr = xr.reshape(batch, seq_len, d)
    params = [w["conv_w"], w["conv_b"]] + [w[k + direction] for k in ("gates_", "ba_", "bi_", "lam_")]
    args = [xr, xr, xr] + params
    in_specs = [halo(-1), tile, halo(halos_per_tile)] + [_resident(p.shape) for p in params]
    if reverse:
        args += [h_fwd, gy.reshape(batch, seq_len, d)]
        in_specs += [tile, tile]
    seg_buf = pltpu.VMEM((RNN_BUFFERS, N_SEGMENTS * SEG_PITCH, RNN_BLOCK), F32)
    scratch = [pltpu.VMEM((RNN_BUFFERS, TILE + 2 * HALO, RNN_BLOCK), F32), seg_buf, seg_buf, seg_buf, seg_buf,
               pltpu.VMEM((1, d), F32)]
    return Stage(
        functools.partial(_rnn_body, seq_len=seq_len, reverse=reverse),
        tuple(args), tuple(in_specs), (tile,),
        (jax.ShapeDtypeStruct((batch, seq_len, d), BF16 if reverse else F32),),
        tuple(scratch),
        init=_rnn_init,
    )


def _mix_body(step, x_ref, at_ref, rn_ref, sga_ref, sgr_ref, wa_ref, wr_ref, wo_ref, o_ref, m_s):
    del step
    n_chunks = D_MODEL // MIX_CHUNK

    def merge(c):
        cols = slice(c * MIX_CHUNK, (c + 1) * MIX_CHUNK)
        attn = _dot(at_ref[...], wa_ref[:, cols])
        rnn = _dot(rn_ref[...], wr_ref[:, cols])
        m_s[:, cols] = (sga_ref[:, cols].astype(F32) * attn + sgr_ref[:, cols].astype(F32) * rnn).astype(BF16)

    def project(c):
        cols = slice(c * MIX_CHUNK, (c + 1) * MIX_CHUNK)
        o_ref[:, cols] = x_ref[:, cols] + _dot(m_s[...], wo_ref[:, cols])

    return ([functools.partial(merge, c) for c in range(n_chunks)]
            + [functools.partial(project, c) for c in range(n_chunks)])


def _mix_stage(x, attn, rnn, sga, sgr, n_tokens, w):
    d = x.shape[-1]
    weights = (w["w_attn_o"], w["w_rnn_o"], w["w_out"])
    return Stage(
        _mix_body,
        (x, attn.reshape(n_tokens, -1), rnn.reshape(n_tokens, -1), sga, sgr) + weights,
        (_rows(d),) * 5 + tuple(_resident(m.shape) for m in weights),
        (_rows(d),),
        (jax.ShapeDtypeStruct((n_tokens, d), F32),),
        (pltpu.VMEM((TILE, d), BF16),),
    )


def _layer(x_prompt, x_sample, w, g_final):
    d = x_prompt.shape[-1]
    n_tokens = x_prompt.shape[0] * x_prompt.shape[1]
    half_batch = x_sample.shape[0] // 2
    assert half_batch * x_sample.shape[1] == n_tokens and n_tokens % TILE == 0
    n_steps = n_tokens // TILE
    parts = (
        (x_prompt.reshape(-1, d), 0, x_prompt.shape[0], x_prompt.shape[1]),
        (x_sample.reshape(-1, d), 0, half_batch, x_sample.shape[1]),
        (x_sample.reshape(-1, d), n_steps, half_batch, x_sample.shape[1]),
    )
    st = [dict() for _ in range(N_PARTS)]

    def ffn1(p):
        x, first, _, _ = parts[p]
        return _ffn_stage(x, n_tokens, w, "ffn1", first_tile=first)

    def inproj(p):
        return _inproj_stage(st[p]["x1"], n_tokens, w)

    def attn(p):
        return _attn_stage(st[p]["q"], st[p]["kv"], parts[p][2], parts[p][3], w)

    def rnn_f(p):
        return _rnn_stage(st[p]["xr"], parts[p][2], parts[p][3], w, "f")

    def rnn_b(p):
        return _rnn_stage(st[p]["xr"], parts[p][2], parts[p][3], w, "b", h_fwd=st[p]["hf"], gy=st[p]["gy"])

    def mix(p):
        s = st[p]
        return _mix_stage(s["x1"], s["attn"], s["rnn"], s["sga"], s["sgr"], n_tokens, w)

    def ffn2(p, y_prev=None):
        return _ffn_stage(st[p]["x2"], n_tokens, w, "ffn2", g_final=g_final, y_prev=y_prev)

    produces = {ffn1: ("x1",), inproj: ("q", "kv", "xr", "gy", "sga", "sgr"), attn: ("attn",), rnn_f: ("hf",),
                rnn_b: ("rnn",), mix: ("x2",), ffn2: ("y",)}

    def call(name, *work, **kw):
        stages = [fn(p, **kw) if fn is ffn2 else fn(p) for fn, p in work]
        for (fn, p), outs in zip(work, _run(name, n_steps, *stages)):
            st[p].update(zip(produces[fn], outs))

    call("ffn1_p0", (ffn1, 0))
    call("inproj_p0", (inproj, 0))
    call("ffn1_p1__attn_p0", (ffn1, 1), (attn, 0))
    call("inproj_p1__rnnf_p0", (inproj, 1), (rnn_f, 0))
    call("ffn1_p2__rnnb_p0", (ffn1, 2), (rnn_b, 0))
    call("inproj_p2__attn_p1", (inproj, 2), (attn, 1))
    call("mix_p0__rnnf_p1", (mix, 0), (rnn_f, 1))
    call("ffn2_p0__rnnb_p1", (ffn2, 0), (rnn_b, 1))
    call("mix_p1__attn_p2", (mix, 1), (attn, 2))
    call("ffn2_p1__rnnf_p2", (ffn2, 1), (rnn_f, 2))
    call("rnnb_p2", (rnn_b, 2))
    call("mix_p2", (mix, 2))
    call("ffn2_p2", (ffn2, 2), y_prev=st[1]["y"])
    return st[0]["y"].reshape(x_prompt.shape), st[2]["y"].reshape(x_sample.shape)


def kernel(x_prompt, x_sample, norm_ffn1, ffn1_gate, ffn1_up, ffn1_down, norm_mix, w_in, attn_sink, conv_w, conv_b, lru_wa_f, lru_ba_f, lru_wi_f, lru_bi_f, lru_lam_f, lru_wa_b, lru_ba_b, lru_wi_b, lru_bi_b, lru_lam_b, w_attn_o, w_rnn_o, w_out, norm_ffn2, ffn2_gate, ffn2_up, ffn2_down, norm_final):
    depth = norm_ffn1.shape[0]
    for l in range(depth):
        w = {
            "norm_ffn1": norm_ffn1[l][None], "norm_mix": norm_mix[l][None], "norm_ffn2": norm_ffn2[l][None],
            "ffn1_gate": ffn1_gate[l].astype(BF16), "ffn1_up": ffn1_up[l].astype(BF16),
            "ffn1_down": ffn1_down[l].astype(BF16),
            "ffn2_gate": ffn2_gate[l].astype(BF16), "ffn2_up": ffn2_up[l].astype(BF16),
            "ffn2_down": ffn2_down[l].astype(BF16),
            "w_in": w_in[l].astype(BF16),
            "attn_sink": attn_sink[l], "attn_bias": _attn_bias(),
            "conv_w": conv_w[l], "conv_b": conv_b[l][None],
            "gates_f": jnp.concatenate([lru_wa_f[l], lru_wi_f[l]], axis=-1).astype(BF16),
            "gates_b": jnp.concatenate([lru_wa_b[l], lru_wi_b[l]], axis=-1).astype(BF16),
            "ba_f": lru_ba_f[l][None], "bi_f": lru_bi_f[l][None], "lam_f": lru_lam_f[l][None],
            "ba_b": lru_ba_b[l][None], "bi_b": lru_bi_b[l][None], "lam_b": lru_lam_b[l][None],
            "w_attn_o": w_attn_o[l].astype(BF16), "w_rnn_o": w_rnn_o[l].astype(BF16),
            "w_out": w_out[l].astype(BF16),
        }
        g_final = norm_final[None] if l == depth - 1 else None
        x_prompt, x_sample = _layer(x_prompt, x_sample, w, g_final)
    return (x_prompt, x_sample)
```

```python
import functools
import math
from typing import Any, Callable, NamedTuple

import jax
import jax.numpy as jnp
from jax import lax
from jax.experimental import pallas as pl
from jax.experimental.pallas import tpu as pltpu

D_MODEL = 1024
N_HEADS = 8
N_KV_HEADS = 2
HEAD_DIM = 128
GROUP = N_HEADS // N_KV_HEADS
WINDOW = 128
Q_W = N_HEADS * HEAD_DIM
KV_W = N_KV_HEADS * HEAD_DIM
NEG_INF = -1e30
D_RNN = 1024
N_RNN_BLOCKS = 8
RNN_BLOCK = D_RNN // N_RNN_BLOCKS
CONV_WIDTH = 4
CONV_LEFT = 2
LRU_C = 8.0
D_FF = 2816
EPS = 1e-6
LOG2E = math.log2(math.e)

V7X_SUBLANES = 8
V7X_MXU_DIM = 256
V7X_VMEM_LIMIT_BYTES = 56 * 1024 * 1024

N_PARTS = 3
TILE = 512
WIDE_TILE = 1024
FF_CHUNK = V7X_MXU_DIM
INPROJ_CHUNK = 2 * V7X_MXU_DIM
MIX_CHUNK = 2 * V7X_MXU_DIM
FFN_OUT_CHUNK = V7X_MXU_DIM
Q_SUB = 256
SPAN = Q_SUB + 2 * WINDOW
HALO = V7X_SUBLANES
SCAN_PARTS = 4
N_SEGMENTS = V7X_SUBLANES
SEG_LEN = TILE // N_SEGMENTS
SEG_PITCH = SEG_LEN + 4
RNN_BUFFERS = 2
SQRT_FLOOR = 1e-30

BF16 = jnp.bfloat16
F32 = jnp.float32


class Stage(NamedTuple):
    body: Callable[..., list]
    args: tuple
    in_specs: tuple
    out_specs: tuple
    out_shapes: tuple
    scratch: tuple = ()
    init: Any = None


def _run(name, n_steps, *stages):
    counts = [(len(s.args), len(s.out_shapes), len(s.scratch)) for s in stages]
    n_in = sum(c[0] for c in counts)
    n_out = sum(c[1] for c in counts)

    def kernel(*refs):
        step = pl.program_id(0)
        if any(s.init is not None for s in stages):
            @pl.when(step == 0)
            def _():
                i0, o0, s0 = 0, n_in, n_in + n_out
                for stage, (ni, no, ns) in zip(stages, counts):
                    if stage.init is not None:
                        stage.init(*refs[i0:i0 + ni], *refs[o0:o0 + no], *refs[s0:s0 + ns])
                    i0, o0, s0 = i0 + ni, o0 + no, s0 + ns
        i0, o0, s0 = 0, n_in, n_in + n_out
        slices = []
        for stage, (ni, no, ns) in zip(stages, counts):
            pieces = stage.body(step, *refs[i0:i0 + ni], *refs[o0:o0 + no], *refs[s0:s0 + ns])
            slices += [((k + 0.5) / len(pieces), piece) for k, piece in enumerate(pieces)]
            i0, o0, s0 = i0 + ni, o0 + no, s0 + ns
        for _, piece in sorted(slices, key=lambda item: item[0]):
            piece()

    outs = pl.pallas_call(
        kernel,
        grid=(n_steps,),
        in_specs=[spec for s in stages for spec in s.in_specs],
        out_specs=[spec for s in stages for spec in s.out_specs],
        out_shape=[shape for s in stages for shape in s.out_shapes],
        scratch_shapes=[scr for s in stages for scr in s.scratch],
        compiler_params=pltpu.CompilerParams(
            dimension_semantics=("arbitrary",), vmem_limit_bytes=V7X_VMEM_LIMIT_BYTES),
        name=name,
    )(*[a for s in stages for a in s.args])
    result, k = [], 0
    for _, no, _ in counts:
        result.append(outs[k:k + no])
        k += no
    return result


def _resident(shape):
    return pl.BlockSpec(shape, lambda i: (0,) * len(shape), pipeline_mode=pl.Buffered(1))


def _rows(width, tile, first_tile=0):
    return pl.BlockSpec((tile, width), lambda i: (first_tile + i, 0))


def _rmsnorm(x, g):
    y = x * lax.rsqrt(jnp.mean(x * x, axis=-1, keepdims=True) + EPS)
    return y * g


def _dot(a, b):
    return jnp.dot(a, b, preferred_element_type=F32)


def _ffn_body(step, x_ref, g_ref, wg_ref, wu_ref, wd_ref, *rest, final_norm, passthrough):
    del step
    rest = list(rest)
    gf_ref = rest.pop(0) if final_norm else None
    y_prev_ref = rest.pop(0) if passthrough else None
    o_ref, h_s, act_s = rest
    y_ref = o_ref.at[1] if passthrough else o_ref

    def start():
        h_s[...] = _rmsnorm(x_ref[...], g_ref[...]).astype(BF16)

    def expand(c):
        cols = slice(c * FF_CHUNK, (c + 1) * FF_CHUNK)
        gate = _dot(h_s[...], wg_ref[:, cols])
        up = _dot(h_s[...], wu_ref[:, cols])
        act_s[:, cols] = (gate * _sigmoid(gate) * up).astype(BF16)

    def contract(c):
        cols = slice(c * FFN_OUT_CHUNK, (c + 1) * FFN_OUT_CHUNK)
        y_ref[:, cols] = x_ref[:, cols] + 0.5 * _dot(act_s[...], wd_ref[:, cols])

    def finish():
        if final_norm:
            y_ref[...] = _rmsnorm(y_ref[...], gf_ref[...])
        if passthrough:
            o_ref[0] = y_prev_ref[...]

    pieces = [start] + [functools.partial(expand, c) for c in range(D_FF // FF_CHUNK)]
    pieces += [functools.partial(contract, c) for c in range(D_MODEL // FFN_OUT_CHUNK)]
    return pieces + [finish]


def _ffn_stage(x, n_tokens, w, prefix, tile, first_token=0, g_final=None, y_prev=None):
    d = x.shape[-1]
    wg, wu, wd = w[prefix + "_gate"], w[prefix + "_up"], w[prefix + "_down"]
    args = [x, w["norm_" + prefix], wg, wu, wd]
    in_specs = [_rows(d, tile, first_token // tile), _resident((1, d)), _resident(wg.shape), _resident(wu.shape),
                _resident(wd.shape)]
    if g_final is not None:
        args.append(g_final)
        in_specs.append(_resident((1, d)))
    if y_prev is not None:
        args.append(y_prev)
        in_specs.append(_rows(d, tile))
        out_spec = pl.BlockSpec((2, tile, d), lambda i: (0, i, 0))
        out_shape = jax.ShapeDtypeStruct((2, n_tokens, d), F32)
    else:
        out_spec = _rows(d, tile)
        out_shape = jax.ShapeDtypeStruct((n_tokens, d), F32)
    body = functools.partial(_ffn_body, final_norm=g_final is not None, passthrough=y_prev is not None)
    scratch = (pltpu.VMEM((tile, d), BF16), pltpu.VMEM((tile, D_FF), BF16))
    return Stage(body, tuple(args), tuple(in_specs), (out_spec,), (out_shape,), scratch)


_INPROJ_WIDTHS = (Q_W, 2 * KV_W, D_RNN, D_RNN, D_MODEL, D_MODEL)
_INPROJ_DTYPES = (BF16, BF16, F32, BF16, BF16, BF16)


def _inproj_body(step, x_ref, g_ref, w_ref, q_ref, kv_ref, xr_ref, gy_ref, sga_ref, sgr_ref, h_s):
    del step

    def start():
        h_s[...] = _rmsnorm(x_ref[...], g_ref[...]).astype(BF16)

    def project(ref, fn, col, lo, width):
        ref[:, lo:lo + width] = fn(_dot(h_s[...], w_ref[:, col + lo:col + lo + width])).astype(ref.dtype)

    post = (
        lambda q: q * (HEAD_DIM ** -0.5 * LOG2E),
        lambda kv: kv,
        lambda xr: xr,
        jax.nn.gelu,
        _sigmoid,
        _sigmoid,
    )
    pieces = [start]
    col = 0
    for ref, fn in zip((q_ref, kv_ref, xr_ref, gy_ref, sga_ref, sgr_ref), post):
        for lo in range(0, ref.shape[-1], INPROJ_CHUNK):
            pieces.append(functools.partial(project, ref, fn, col, lo, INPROJ_CHUNK))
        col += ref.shape[-1]
    return pieces


def _inproj_stage(x, n_tokens, w, tile):
    d = x.shape[-1]
    return Stage(
        _inproj_body, (x, w["norm_mix"], w["w_in"]),
        (_rows(d, tile), _resident((1, d)), _resident(w["w_in"].shape)),
        tuple(_rows(width, tile) for width in _INPROJ_WIDTHS),
        tuple(jax.ShapeDtypeStruct((n_tokens, width), t) for width, t in zip(_INPROJ_WIDTHS, _INPROJ_DTYPES)),
        (pltpu.VMEM((tile, d), BF16),),
    )


def _attn_body(step, sink_ref, q_ref, kvp_ref, kvc_ref, kvn_ref, bias_ref, o_ref, *, seq_len):
    tiles_per_seq = seq_len // TILE
    j = step % tiles_per_seq
    n_sub = TILE // Q_SUB
    v = {}

    def start():
        v["kv"] = jnp.concatenate([kvp_ref[...], kvc_ref[...], kvn_ref[...]], axis=0)
        lane = lax.broadcasted_iota(jnp.int32, (1, WINDOW), 1)
        v["edge_lo"] = jnp.where(j * TILE - WINDOW + lane >= 0, 0.0, NEG_INF).astype(F32)
        v["edge_hi"] = jnp.where((j + 1) * TILE + lane < seq_len, 0.0, NEG_INF).astype(F32)

    def group(sub, kh):
        rows = slice(sub * Q_SUB, (sub + 1) * Q_SUB)
        span = v["kv"][sub * Q_SUB:sub * Q_SUB + SPAN]
        k = span[:, kh * HEAD_DIM:(kh + 1) * HEAD_DIM]
        val = span[:, KV_W + kh * HEAD_DIM:KV_W + (kh + 1) * HEAD_DIM]
        heads = range(kh * GROUP, (kh + 1) * GROUP)
        q = jnp.concatenate([q_ref[rows, h * HEAD_DIM:(h + 1) * HEAD_DIM] for h in heads], axis=0)
        s_all = lax.dot_general(q, k, (((1,), (1,)), ((), ())), preferred_element_type=F32)
        probs, denoms = [], []
        for g, h in enumerate(heads):
            s = s_all[g * Q_SUB:(g + 1) * Q_SUB] + bias_ref[h]
            if sub == 0:
                s = jnp.concatenate([s[:, :WINDOW] + v["edge_lo"], s[:, WINDOW:]], axis=1)
            if sub == n_sub - 1:
                s = jnp.concatenate([s[:, :SPAN - WINDOW], s[:, SPAN - WINDOW:] + v["edge_hi"]], axis=1)
            sink = sink_ref[h] * LOG2E
            m = jnp.maximum(jnp.max(s, axis=-1, keepdims=True), sink)
            p = jnp.exp2(s - m)
            denoms.append(jnp.sum(p, axis=-1, keepdims=True) + jnp.exp2(sink - m))
            probs.append(p.astype(BF16))
        o_all = _dot(jnp.concatenate(probs, axis=0), val)
        for g, h in enumerate(heads):
            o = o_all[g * Q_SUB:(g + 1) * Q_SUB] / denoms[g]
            o_ref[rows, h * HEAD_DIM:(h + 1) * HEAD_DIM] = o.astype(o_ref.dtype)

    return [start] + [functools.partial(group, sub, kh) for sub in range(n_sub) for kh in range(N_KV_HEADS)]


def _attn_bias():
    qi = jnp.arange(Q_SUB)[:, None]
    kc = jnp.arange(SPAN)[None, :]
    dist = jnp.abs(qi + WINDOW - kc)
    slopes = 2.0 ** (-8.0 * jnp.arange(1, N_HEADS + 1, dtype=F32) / N_HEADS)
    bias = -slopes[:, None, None] * dist.astype(F32)[None] * LOG2E
    return jnp.where((dist <= WINDOW)[None], bias, NEG_INF).astype(F32)


def _attn_stage(q, kv, batch, seq_len, w):
    tiles_per_seq = seq_len // TILE
    halos_per_tile = TILE // WINDOW
    last_halo = seq_len // WINDOW - 1

    def bj(i):
        return i // tiles_per_seq, i % tiles_per_seq

    def tile(width):
        return pl.BlockSpec((None, TILE, width), lambda i: (*bj(i), 0))

    def halo(offset):
        def index(i):
            b, j = bj(i)
            return b, jnp.clip(j * halos_per_tile + offset, 0, last_halo), 0
        return pl.BlockSpec((None, WINDOW, 2 * KV_W), index)

    q = q.reshape(batch, seq_len, Q_W)
    kv = kv.reshape(batch, seq_len, 2 * KV_W)
    bias = w["attn_bias"]
    return Stage(
        functools.partial(_attn_body, seq_len=seq_len),
        (w["attn_sink"], q, kv, kv, kv, bias),
        (pl.BlockSpec(memory_space=pltpu.SMEM), tile(Q_W), halo(-1), tile(2 * KV_W), halo(halos_per_tile),
         _resident(bias.shape)),
        (tile(Q_W),),
        (jax.ShapeDtypeStruct((batch, seq_len, Q_W), BF16),),
    )


def _softplus(x):
    return jnp.maximum(x, 0.0) + jnp.log1p(jnp.exp(-jnp.abs(x)))


def _sigmoid(x):
    return 0.5 * jnp.tanh(0.5 * x) + 0.5


def _sqrt_nonneg(y):
    return y * lax.rsqrt(jnp.maximum(y, SQRT_FLOOR))


def _rnn_body(step, xp_ref, xc_ref, xn_ref, cw_ref, cb_ref, w_ref, ba_ref, bi_ref, lam_ref, *rest,
              seq_len, reverse):
    if reverse:
        hf_ref, gy_ref, o_ref, ext_s, a_s, u_s, h_s, p_s, carry = rest
    else:
        o_ref, ext_s, a_s, u_s, h_s, p_s, carry = rest
    n_tiles = seq_len // TILE
    j = step % n_tiles
    tile = (n_tiles - 1 - j) if reverse else j
    v = {}

    def seg_rows(s):
        return slice(s * SEG_PITCH, s * SEG_PITCH + SEG_LEN)

    def gates(n):
        cols = slice(n * RNN_BLOCK, (n + 1) * RNN_BLOCK)
        buf = n % RNN_BUFFERS
        ext_s[buf, 0:HALO] = jnp.where(tile > 0, xp_ref[:, cols], 0.0)
        ext_s[buf, HALO:HALO + TILE] = xc_ref[:, cols]
        ext_s[buf, HALO + TILE:] = jnp.where(tile < n_tiles - 1, xn_ref[:, cols], 0.0)
        xb = cb_ref[:, cols]
        for tap in range(CONV_WIDTH):
            lo = HALO - CONV_LEFT + tap
            xb = xb + ext_s[buf, lo:lo + TILE] * cw_ref[tap:tap + 1, cols]

        decay_rate = (-LRU_C * LOG2E) * _softplus(-lam_ref[:, cols])
        pre = _dot(xb.astype(BF16), w_ref[n])
        r = _sigmoid(pre[:, :RNN_BLOCK] + ba_ref[:, cols])
        ig = _sigmoid(pre[:, RNN_BLOCK:] + bi_ref[:, cols])
        a = jnp.exp2(r * decay_rate)
        u = _sqrt_nonneg(1.0 - a * a) * (ig * xb)
        for s in range(N_SEGMENTS):
            a_s[buf, seg_rows(s)] = a[s * SEG_LEN:(s + 1) * SEG_LEN]
            u_s[buf, seg_rows(s)] = u[s * SEG_LEN:(s + 1) * SEG_LEN]
        v["h"] = jnp.zeros((N_SEGMENTS, RNN_BLOCK), F32)
        v["p"] = jnp.ones((N_SEGMENTS, RNN_BLOCK), F32)

    def scan(n, part):
        cols = slice(n * RNN_BLOCK, (n + 1) * RNN_BLOCK)
        buf = n % RNN_BUFFERS
        per_part = SEG_LEN // SCAN_PARTS
        h, p = v["h"], v["p"]
        for i in range(part * per_part, (part + 1) * per_part):
            k = (SEG_LEN - 1 - i) if reverse else i
            rows = pl.ds(k, N_SEGMENTS, stride=SEG_PITCH)
            a = a_s[buf, rows]
            h = a * h + u_s[buf, rows]
            p = a * p
            h_s[buf, rows] = h
            p_s[buf, rows] = p
        v["h"], v["p"] = h, p
        if part < SCAN_PARTS - 1:
            return
        state = jnp.where(j == 0, 0.0, carry[:, cols])
        order = range(N_SEGMENTS - 1, -1, -1) if reverse else range(N_SEGMENTS)
        entering = [None] * N_SEGMENTS
        for s in order:
            entering[s] = state
            state = p[s:s + 1] * state + h[s:s + 1]
        carry[:, cols] = state
        for s in range(N_SEGMENTS):
            out_rows = slice(s * SEG_LEN, (s + 1) * SEG_LEN)
            h_seg = h_s[buf, seg_rows(s)] + p_s[buf, seg_rows(s)] * entering[s]
            if reverse:
                h_seg = (hf_ref[out_rows, cols] + h_seg) * gy_ref[out_rows, cols].astype(F32)
            o_ref[out_rows, cols] = h_seg.astype(o_ref.dtype)

    pieces = []
    for n in range(N_RNN_BLOCKS):
        pieces.append(functools.partial(gates, n))
        pieces += [functools.partial(scan, n, part) for part in range(SCAN_PARTS)]
    return pieces


def _rnn_init(*refs):
    carry = refs[-1]
    carry[...] = jnp.zeros_like(carry)


def _rnn_stage(xr, batch, seq_len, w, direction, h_fwd=None, gy=None):
    d = D_RNN
    reverse = h_fwd is not None
    n_tiles = seq_len // TILE
    halos_per_tile = TILE // HALO
    last_halo = seq_len // HALO - 1

    def bt(i):
        j = i % n_tiles
        return i // n_tiles, (n_tiles - 1 - j) if reverse else j

    tile = pl.BlockSpec((None, TILE, d), lambda i: (*bt(i), 0))

    def halo(offset):
        def index(i):
            b, t = bt(i)
            return b, jnp.clip(t * halos_per_tile + offset, 0, last_halo), 0
        return pl.BlockSpec((None, HALO, d), index)

    xr = xr.reshape(batch, seq_len, d)
    params = [w["conv_w"], w["conv_b"]] + [w[k + direction] for k in ("gates_", "ba_", "bi_", "lam_")]
    args = [xr, xr, xr] + params
    in_specs = [halo(-1), tile, halo(halos_per_tile)] + [_resident(p.shape) for p in params]
    if reverse:
        args += [h_fwd, gy.reshape(batch, seq_len, d)]
        in_specs += [tile, tile]
    seg_buf = pltpu.VMEM((RNN_BUFFERS, N_SEGMENTS * SEG_PITCH, RNN_BLOCK), F32)
    scratch = [pltpu.VMEM((RNN_BUFFERS, TILE + 2 * HALO, RNN_BLOCK), F32), seg_buf, seg_buf, seg_buf, seg_buf,
               pltpu.VMEM((1, d), F32)]
    return Stage(
        functools.partial(_rnn_body, seq_len=seq_len, reverse=reverse),
        tuple(args), tuple(in_specs), (tile,),
        (jax.ShapeDtypeStruct((batch, seq_len, d), BF16 if reverse else F32),),
        tuple(scratch),
        init=_rnn_init,
    )


def _mix_body(step, x_ref, at_ref, rn_ref, sga_ref, sgr_ref, wa_ref, wr_ref, wo_ref, o_ref, m_s):
    del step
    n_chunks = D_MODEL // MIX_CHUNK

    def merge(c):
        cols = slice(c * MIX_CHUNK, (c + 1) * MIX_CHUNK)
        attn = _dot(at_ref[...], wa_ref[:, cols])
        rnn = _dot(rn_ref[...], wr_ref[:, cols])
        m_s[:, cols] = (sga_ref[:, cols].astype(F32) * attn + sgr_ref[:, cols].astype(F32) * rnn).astype(BF16)

    def project(c):
        cols = slice(c * MIX_CHUNK, (c + 1) * MIX_CHUNK)
        o_ref[:, cols] = x_ref[:, cols] + _dot(m_s[...], wo_ref[:, cols])

    return ([functools.partial(merge, c) for c in range(n_chunks)]
            + [functools.partial(project, c) for c in range(n_chunks)])


def _mix_stage(x, attn, rnn, sga, sgr, n_tokens, w, tile):
    d = x.shape[-1]
    weights = (w["w_attn_o"], w["w_rnn_o"], w["w_out"])
    return Stage(
        _mix_body,
        (x, attn.reshape(n_tokens, -1), rnn.reshape(n_tokens, -1), sga, sgr) + weights,
        (_rows(d, tile),) * 5 + tuple(_resident(m.shape) for m in weights),
        (_rows(d, tile),),
        (jax.ShapeDtypeStruct((n_tokens, d), F32),),
        (pltpu.VMEM((tile, d), BF16),),
    )


def _layer(x_prompt, x_sample, w, g_final):
    d = x_prompt.shape[-1]
    n_tokens = x_prompt.shape[0] * x_prompt.shape[1]
    half_batch = x_sample.shape[0] // 2
    assert half_batch * x_sample.shape[1] == n_tokens and n_tokens % WIDE_TILE == 0
    parts = (
        (x_sample.reshape(-1, d), 0, half_batch, x_sample.shape[1]),
        (x_sample.reshape(-1, d), n_tokens, half_batch, x_sample.shape[1]),
        (x_prompt.reshape(-1, d), 0, x_prompt.shape[0], x_prompt.shape[1]),
    )
    st = [dict() for _ in range(N_PARTS)]

    def ffn1(p, tile):
        x, first, _, _ = parts[p]
        return _ffn_stage(x, n_tokens, w, "ffn1", tile, first_token=first)

    def inproj(p, tile):
        return _inproj_stage(st[p]["x1"], n_tokens, w, tile)

    def attn(p, tile):
        assert tile == TILE
        return _attn_stage(st[p]["q"], st[p]["kv"], parts[p][2], parts[p][3], w)

    def rnn_f(p, tile):
        assert tile == TILE
        return _rnn_stage(st[p]["xr"], parts[p][2], parts[p][3], w, "f")

    def rnn_b(p, tile):
        assert tile == TILE
        return _rnn_stage(st[p]["xr"], parts[p][2], parts[p][3], w, "b", h_fwd=st[p]["hf"], gy=st[p]["gy"])

    def mix(p, tile):
        s = st[p]
        return _mix_stage(s["x1"], s["attn"], s["rnn"], s["sga"], s["sgr"], n_tokens, w, tile)

    def ffn2(p, tile):
        y_prev = st[p - 1]["y"] if p == 1 else None
        return _ffn_stage(st[p]["x2"], n_tokens, w, "ffn2", tile, g_final=g_final, y_prev=y_prev)

    produces = {ffn1: ("x1",), inproj: ("q", "kv", "xr", "gy", "sga", "sgr"), attn: ("attn",), rnn_f: ("hf",),
                rnn_b: ("rnn",), mix: ("x2",), ffn2: ("y",)}

    def call(name, *work):
        tile = TILE if len(work) > 1 else WIDE_TILE
        stages = [fn(p, tile) for fn, p in work]
        for (fn, p), outs in zip(work, _run(name, n_tokens // tile, *stages)):
            st[p].update(zip(produces[fn], outs))

    call("ffn1_p0", (ffn1, 0))
    call("inproj_p0", (inproj, 0))
    call("ffn1_p1__attn_p0", (ffn1, 1), (attn, 0))
    call("inproj_p1__rnnf_p0", (inproj, 1), (rnn_f, 0))
    call("ffn1_p2__rnnb_p0__rnnf_p1", (ffn1, 2), (rnn_b, 0), (rnn_f, 1))
    call("inproj_p2__attn_p1", (inproj, 2), (attn, 1))
    call("mix_p0", (mix, 0))
    call("ffn2_p0__rnnb_p1__rnnf_p2", (ffn2, 0), (rnn_b, 1), (rnn_f, 2))
    call("mix_p1__attn_p2", (mix, 1), (attn, 2))
    call("ffn2_p1__rnnb_p2", (ffn2, 1), (rnn_b, 2))
    call("mix_p2", (mix, 2))
    call("ffn2_p2", (ffn2, 2))
    return st[2]["y"].reshape(x_prompt.shape), st[1]["y"].reshape(x_sample.shape)


def kernel(x_prompt, x_sample, norm_ffn1, ffn1_gate, ffn1_up, ffn1_down, norm_mix, w_in, attn_sink, conv_w, conv_b, lru_wa_f, lru_ba_f, lru_wi_f, lru_bi_f, lru_lam_f, lru_wa_b, lru_ba_b, lru_wi_b, lru_bi_b, lru_lam_b, w_attn_o, w_rnn_o, w_out, norm_ffn2, ffn2_gate, ffn2_up, ffn2_down, norm_final):
    depth = norm_ffn1.shape[0]
    for l in range(depth):
        w = {
            "norm_ffn1": norm_ffn1[l][None], "norm_mix": norm_mix[l][None], "norm_ffn2": norm_ffn2[l][None],
            "ffn1_gate": ffn1_gate[l].astype(BF16), "ffn1_up": ffn1_up[l].astype(BF16),
            "ffn1_down": ffn1_down[l].astype(BF16),
            "ffn2_gate": ffn2_gate[l].astype(BF16), "ffn2_up": ffn2_up[l].astype(BF16),
            "ffn2_down": ffn2_down[l].astype(BF16),
            "w_in": w_in[l].astype(BF16),
            "attn_sink": attn_sink[l], "attn_bias": _attn_bias(),
            "conv_w": conv_w[l], "conv_b": conv_b[l][None],
            "gates_f": jnp.concatenate([lru_wa_f[l], lru_wi_f[l]], axis=-1).astype(BF16),
            "gates_b": jnp.concatenate([lru_wa_b[l], lru_wi_b[l]], axis=-1).astype(BF16),
            "ba_f": lru_ba_f[l][None], "bi_f": lru_bi_f[l][None], "lam_f": lru_lam_f[l][None],
            "ba_b": lru_ba_b[l][None], "bi_b": lru_bi_b[l][None], "lam_b": lru_lam_b[l][None],
            "w_attn_o": w_attn_o[l].astype(BF16), "w_rnn_o": w_rnn_o[l].astype(BF16),
            "w_out": w_out[l].astype(BF16),
        }
        g_final = norm_final[None] if l == depth - 1 else None
        x_prompt, x_sample = _layer(x_prompt, x_sample, w, g_final)
    return (x_prompt, x_sample)
```
